```python
import jax
import jax.numpy as jnp
from jax import lax
import numpy as np

D_MODEL = 1024
BATCH = 16
SEQ = 4096
DEPTH = 4

N_MEM = 256
HEAD_DIM = 64
CONV_CH = D_MODEL // 2
CONV_GROUPS = CONV_CH // HEAD_DIM
CONV_K = 3
FOX_HEADS = (D_MODEL // 2) // HEAD_DIM
FOX_WIDTH = FOX_HEADS * HEAD_DIM
MIX_WIDTH = CONV_CH + FOX_WIDTH
IN_COLS = 3 * CONV_CH + 3 * FOX_WIDTH + FOX_HEADS
XA_HEADS = 4
XA_HEAD_DIM = 128
XA_WIDTH = XA_HEADS * XA_HEAD_DIM
D_FF = 2816
N_EXPERTS = 8
TOP_K = 2
Q_BLOCK = 128
EPS = 1e-6
N_DENSE = (DEPTH + 1) // 2
N_MOE = DEPTH // 2
SPLITS = (CONV_CH, 2 * CONV_CH, 3 * CONV_CH,
          3 * CONV_CH + FOX_WIDTH, 3 * CONV_CH + 2 * FOX_WIDTH, 3 * CONV_CH + 3 * FOX_WIDTH)

kernel_name = 'hybrid_conv_fox_memxattn_moe'


def rmsnorm(x, g):
    xf = x.astype(jnp.float32)
    y = xf * lax.rsqrt(jnp.mean(xf * xf, axis=-1, keepdims=True) + EPS)
    return (y * g.astype(jnp.float32)).astype(x.dtype)


def short_gated_conv(gate_b, gate_c, v, w):
    S = v.shape[1]
    u = gate_c * v
    up = jnp.pad(u, ((0, 0), (CONV_K - 1, 0), (0, 0)))
    y = w[0] * up[:, 0:S]
    for j in range(1, CONV_K):
        y = y + w[j] * up[:, j:j + S]
    return gate_b * y


def forgetting_attention(q, k, v, log_f):
    B, S, H, Dh = q.shape
    c = jnp.transpose(jnp.cumsum(log_f, axis=1), (0, 2, 1))
    n_blk = S // Q_BLOCK
    key_pos = jnp.arange(S)
    scale = Dh ** -0.5

    def block(i):
        start = i * Q_BLOCK
        qb = lax.dynamic_slice_in_dim(q, start, Q_BLOCK, axis=1)
        cb = lax.dynamic_slice_in_dim(c, start, Q_BLOCK, axis=2)
        s = jnp.einsum('bqhd,bkhd->bhqk', qb, k, preferred_element_type=jnp.float32) * scale
        s = s + cb[..., :, None] - c[..., None, :]
        q_pos = start + jnp.arange(Q_BLOCK)
        s = jnp.where(key_pos[None, :] <= q_pos[:, None], s, -jnp.inf)
        p = jax.nn.softmax(s, axis=-1).astype(v.dtype)
        return jnp.einsum('bhqk,bkhd->bqhd', p, v)

    out = lax.map(block, jnp.arange(n_blk))
    return jnp.moveaxis(out, 0, 1).reshape(B, S, H, Dh)


def mem_cross_attention(h, m, w_q, w_kv, g_q, g_k):
    B, S, _ = h.shape
    M = m.shape[1]
    q = rmsnorm((h @ w_q).reshape(B, S, XA_HEADS, XA_HEAD_DIM), g_q)
    kv = (m @ w_kv).reshape(B, M, 2, XA_HEADS, XA_HEAD_DIM)
    k = rmsnorm(kv[:, :, 0], g_k)
    v = kv[:, :, 1]
    s = jnp.einsum('bshd,bmhd->bhsm', q, k, preferred_element_type=jnp.float32) * (XA_HEAD_DIM ** -0.5)
    p = jax.nn.softmax(s, axis=-1).astype(v.dtype)
    return jnp.einsum('bhsm,bmhd->bshd', p, v).reshape(B, S, XA_WIDTH)


def swiglu(h, w_g, w_u, w_d):
    return (jax.nn.silu(h @ w_g) * (h @ w_u)) @ w_d


def moe_swiglu(h, w_r, b_r, w_g, w_u, w_d):
    B, S, D = h.shape
    t = h.reshape(B * S, D)
    logits = jnp.dot(t, w_r, preferred_element_type=jnp.float32) + b_r.astype(jnp.float32)
    top_v, top_i = lax.top_k(logits, TOP_K)
    top_w = jax.nn.softmax(top_v, axis=-1)
    gates = jnp.sum(jax.nn.one_hot(top_i, N_EXPERTS, dtype=jnp.float32) * top_w[..., None], axis=1)
    gates = gates.astype(h.dtype)
    out = jnp.zeros_like(t)
    for e in range(N_EXPERTS):
        out = out + gates[:, e:e + 1] * swiglu(t, w_g[e], w_u[e], w_d[e])
    return out.reshape(B, S, D)


def setup_inputs(seed: int = 0) -> dict:
    key = jax.random.key(seed)
    ks = iter(jax.random.split(key, 32))
    L = DEPTH

    def nrm(shape, fan_in):
        return jax.random.normal(next(ks), shape, jnp.float32) * (fan_in ** -0.5)

    def gain(shape):
        return 1.0 + 0.02 * jax.random.normal(next(ks), shape, jnp.float32)

    return {
        'x': jax.random.normal(next(ks), (BATCH, SEQ, D_MODEL), jnp.float32),
        'mem': jax.random.normal(next(ks), (BATCH, N_MEM, D_MODEL), jnp.float32),
        'g_mix': gain((L, D_MODEL)),
        'w_in': nrm((L, D_MODEL, IN_COLS), D_MODEL),
        'conv_w': nrm((L, CONV_K, CONV_CH), CONV_K),
        'b_f': jax.random.uniform(next(ks), (L, FOX_HEADS), jnp.float32, 1.0, 4.0),
        'g_q': gain((L, HEAD_DIM)),
        'g_k': gain((L, HEAD_DIM)),
        'g_conv_out': gain((L, CONV_CH)),
        'g_fox_out': gain((L, FOX_WIDTH)),
        'w_out': nrm((L, MIX_WIDTH, D_MODEL), MIX_WIDTH),
        'g_xa': gain((L, D_MODEL)),
        'g_mem': gain((L, D_MODEL)),
        'w_xq': nrm((L, D_MODEL, XA_WIDTH), D_MODEL),
        'w_xkv': nrm((L, D_MODEL, 2 * XA_WIDTH), D_MODEL),
        'g_xq': gain((L, XA_HEAD_DIM)),
        'g_xk': gain((L, XA_HEAD_DIM)),
        'w_xo': nrm((L, XA_WIDTH, D_MODEL), XA_WIDTH),
        'g_ffn': gain((L, D_MODEL)),
        'w_gate': nrm((N_DENSE, D_MODEL, D_FF), D_MODEL),
        'w_up': nrm((N_DENSE, D_MODEL, D_FF), D_MODEL),
        'w_down': nrm((N_DENSE, D_FF, D_MODEL), D_FF),
        'w_router': nrm((N_MOE, D_MODEL, N_EXPERTS), D_MODEL),
        'b_router': 0.01 * jax.random.normal(next(ks), (N_MOE, N_EXPERTS), jnp.float32),
        'we_gate': nrm((N_MOE, N_EXPERTS, D_MODEL, D_FF), D_MODEL),
        'we_up': nrm((N_MOE, N_EXPERTS, D_MODEL, D_FF), D_MODEL),
        'we_down': nrm((N_MOE, N_EXPERTS, D_FF, D_MODEL), D_FF),
    }


def reference(x, mem, g_mix, w_in, conv_w, b_f, g_q, g_k, g_conv_out, g_fox_out, w_out,
              g_xa, g_mem, w_xq, w_xkv, g_xq, g_xk, w_xo, g_ffn, w_gate, w_up, w_down,
              w_router, b_router, we_gate, we_up, we_down):
    B, S, _ = x.shape
    for l in range(DEPTH):
        h = rmsnorm(x, g_mix[l])
        z = h @ w_in[l]
        c_b, c_c, c_v, q, k, v, f_logit = jnp.split(z, SPLITS, axis=-1)
        y_conv = short_gated_conv(c_b, c_c, c_v, conv_w[l])
        y_conv = rmsnorm(y_conv.reshape(B, S, CONV_GROUPS, HEAD_DIM),
                         g_conv_out[l].reshape(CONV_GROUPS, HEAD_DIM)).reshape(B, S, CONV_CH)
        q = rmsnorm(q.reshape(B, S, FOX_HEADS, HEAD_DIM), g_q[l])
        k = rmsnorm(k.reshape(B, S, FOX_HEADS, HEAD_DIM), g_k[l])
        v = v.reshape(B, S, FOX_HEADS, HEAD_DIM)
        log_f = jax.nn.log_sigmoid(f_logit.astype(jnp.float32) + b_f[l].astype(jnp.float32))
        y_fox = forgetting_attention(q, k, v, log_f)
        y_fox = rmsnorm(y_fox, g_fox_out[l].reshape(FOX_HEADS, HEAD_DIM)).reshape(B, S, FOX_WIDTH)
        x = x + jnp.concatenate([y_conv, y_fox], axis=-1) @ w_out[l]
        hx = rmsnorm(x, g_xa[l])
        m = rmsnorm(mem, g_mem[l])
        x = x + mem_cross_attention(hx, m, w_xq[l], w_xkv[l], g_xq[l], g_xk[l]) @ w_xo[l]
        hf = rmsnorm(x, g_ffn[l])
        i = l // 2
        if l % 2 == 0:
            x = x + swiglu(hf, w_gate[i], w_up[i], w_down[i])
        else:
            x = x + moe_swiglu(hf, w_router[i], b_router[i], we_gate[i], we_up[i], we_down[i])
    return x
```

```python
import functools

import jax
import jax.numpy as jnp
from jax import lax
from jax.experimental import pallas as pl
from jax.experimental.pallas import tpu as pltpu

D_MODEL = 1024
DEPTH = 4
HEAD_DIM = 64
CONV_CH = 512
CONV_K = 3
FOX_HEADS = 8
FOX_WIDTH = 512
XA_HEADS = 4
XA_HEAD_DIM = 128
XA_WIDTH = 512
D_FF = 2816
N_EXPERTS = 8
EPS = 1e-6

LANES = 128
SUBLANES = 8
MXU_DIM = 256
VMEM_LIMIT = 56 * 1024 * 1024

ROW_TILE = 512
ATT_TILE = 512
FF_CHUNK = 256

F32 = jnp.float32
BF16 = jnp.bfloat16


def _const_spec(shape):
    return pl.BlockSpec(shape, lambda *_: (0,) * len(shape), pipeline_mode=pl.Buffered(1))


def _rms(x, g):
    return x * lax.rsqrt(jnp.mean(x * x, axis=-1, keepdims=True) + EPS) * g


def _split2(x):
    hi = x.astype(BF16)
    lo = (x - hi.astype(F32)).astype(BF16)
    return hi, lo


def _group_rms(y, g, gmat):
    hi, lo = _split2(y * y)
    parts = []
    for c in range(y.shape[1] // MXU_DIM):
        sl = slice(c * MXU_DIM, (c + 1) * MXU_DIM)
        parts.append(jnp.dot(hi[:, sl], gmat, preferred_element_type=F32)
                     + jnp.dot(lo[:, sl], gmat, preferred_element_type=F32))
    ms = jnp.concatenate(parts, axis=1) * (1.0 / HEAD_DIM)
    return y * lax.rsqrt(ms + EPS) * g


def _mix_in_kernel(tiles_per_seq, x_ref, g_ref, w_ref, wf_ref, cw_ref, bf_ref, gq_ref, gk_ref, gc_ref,
                   gmat_ref, tri_ref, yc_ref, q_ref, k_ref, v_ref, c_ref, ubuf, ccarry):
    i = pl.program_id(0)
    tm = x_ref.shape[0]
    h = _rms(x_ref[...], g_ref[...]).astype(BF16)

    def proj(j):
        return jnp.dot(h, w_ref[:, j * CONV_CH:(j + 1) * CONV_CH], preferred_element_type=F32)

    @pl.when(i % tiles_per_seq == 0)
    def _():
        ubuf[0:SUBLANES, :] = jnp.zeros((SUBLANES, CONV_CH), F32)
        ccarry[...] = jnp.zeros_like(ccarry)

    ubuf[SUBLANES:SUBLANES + tm, :] = proj(1) * proj(2)
    y = (cw_ref[0:1, :] * ubuf[SUBLANES - 2:SUBLANES - 2 + tm, :]
         + cw_ref[1:2, :] * ubuf[SUBLANES - 1:SUBLANES - 1 + tm, :]
         + cw_ref[2:3, :] * ubuf[SUBLANES:SUBLANES + tm, :])
    ubuf[0:SUBLANES, :] = ubuf[tm:tm + SUBLANES, :]
    gmat = gmat_ref[...]
    yc_ref[...] = _group_rms(proj(0) * y, gc_ref[...], gmat).astype(BF16)

    q_ref[...] = (_group_rms(proj(3), gq_ref[...], gmat) * (HEAD_DIM ** -0.5)).astype(BF16)
    k_ref[...] = _group_rms(proj(4), gk_ref[...], gmat).astype(BF16)
    v_ref[...] = proj(5).astype(BF16)

    z = jnp.dot(h, wf_ref[...], preferred_element_type=F32) + bf_ref[...]
    logf = jnp.minimum(z, 0.0) - jnp.log1p(jnp.exp(-jnp.abs(z)))
    p1 = logf.astype(BF16)
    r1 = logf - p1.astype(F32)
    p2 = r1.astype(BF16)
    p3 = (r1 - p2.astype(F32)).astype(BF16)
    lane = lax.broadcasted_iota(jnp.int32, logf.shape, 1)
    parts = jnp.where(lane < FOX_HEADS, p1, jnp.where(lane < 2 * FOX_HEADS, p2, p3))
    cs = jnp.dot(tri_ref[...], parts, preferred_element_type=F32)
    cs = cs + pltpu.roll(cs, LANES - FOX_HEADS, axis=1) + pltpu.roll(cs, LANES - 2 * FOX_HEADS, axis=1)
    c = cs + ccarry[...]
    c_ref[...] = c
    ccarry[...] = c[tm - 1:tm, :]


def _mix_in(x, g, w_main, w_f, conv_w, b_f, gq, gk, gc, gmat, tri, seq):
    n = x.shape[0]
    tm = ROW_TILE
    row = lambda width: pl.BlockSpec((tm, width), lambda i: (i, 0))
    out = [jax.ShapeDtypeStruct((n, CONV_CH), BF16)] * 4 + [jax.ShapeDtypeStruct((n, LANES), F32)]
    return pl.pallas_call(
        functools.partial(_mix_in_kernel, seq // tm),
        grid=(n // tm,),
        in_specs=[row(D_MODEL), _const_spec((1, D_MODEL)), _const_spec(w_main.shape), _const_spec(w_f.shape),
                  _const_spec(conv_w.shape), _const_spec((1, LANES)), _const_spec((1, CONV_CH)),
                  _const_spec((1, CONV_CH)), _const_spec((1, CONV_CH)), _const_spec(gmat.shape),
                  _const_spec(tri.shape)],
        out_specs=[row(CONV_CH)] * 4 + [row(LANES)],
        out_shape=out,
        scratch_shapes=[pltpu.VMEM((tm + SUBLANES, CONV_CH), F32), pltpu.VMEM((1, LANES), F32)],
        compiler_params=pltpu.CompilerParams(dimension_semantics=("arbitrary",), vmem_limit_bytes=VMEM_LIMIT),
        name="mix_in",
    )(x, g, w_main, w_f, conv_w, b_f, gq, gk, gc, gmat, tri)


def _fox_kernel(q_ref, k_ref, v_ref, ccol_ref, crow_ref, g_ref, o_ref):
    pair = pl.program_id(1)
    qi = pl.program_id(2)
    tq = q_ref.shape[0]
    tk = ATT_TILE
    q = q_ref[...]
    lane = lax.broadcasted_iota(jnp.int32, (1, LANES), 1)
    causal = (lax.broadcasted_iota(jnp.int32, (tq, tk), 0) >= lax.broadcasted_iota(jnp.int32, (tq, tk), 1))
    outs = []
    for j in range(2):
        head = 2 * pair + j
        in_head = (lane >= j * HEAD_DIM) & (lane < (j + 1) * HEAD_DIM)
        qh = jnp.where(in_head, q, jnp.zeros_like(q))
        ccol = jnp.sum(jnp.where(lane == head, ccol_ref[...], 0.0), axis=1, keepdims=True)

        def scores(kv):
            start = pl.multiple_of(kv * tk, tk)
            kb = k_ref[pl.ds(start, tk), :]
            s = lax.dot_general(qh, kb, (((1,), (1,)), ((), ())), preferred_element_type=F32)
            return s + (ccol - crow_ref[0, pl.ds(head, 1), pl.ds(start, tk)]), start

        def update(s, start, carry):
            m, l, acc = carry
            m_new = jnp.maximum(m, jnp.max(s, axis=1, keepdims=True))
            alpha = jnp.exp(m - m_new)
            p = jnp.exp(s - m_new)
            l = alpha * l + jnp.sum(p, axis=1, keepdims=True)
            acc = alpha * acc + jnp.dot(p.astype(BF16), v_ref[pl.ds(start, tk), :], preferred_element_type=F32)
            return m_new, l, acc

        def body(kv, carry):
            s, start = scores(kv)
            return update(s, start, carry)

        init = (jnp.full((tq, 1), -jnp.inf, F32), jnp.zeros((tq, 1), F32), jnp.zeros((tq, LANES), F32))
        carry = lax.fori_loop(0, qi, body, init)
        s, start = scores(qi)
        _, l, acc = update(jnp.where(causal, s, -jnp.inf), start, carry)
        o = jnp.where(in_head, acc / l, 0.0)
        ms = jnp.sum(o * o, axis=1, keepdims=True) * (1.0 / HEAD_DIM)
        outs.append(o * lax.rsqrt(ms + EPS))
    o_ref[...] = ((outs[0] + outs[1]) * g_ref[...]).astype(BF16)


def _fox_attention(q, k, v, c, c_rows, g_fox, batch, seq):
    n = q.shape[0]
    tq = ATT_TILE
    nq = seq // tq
    qspec = pl.BlockSpec((tq, LANES), lambda b, p, i: (b * nq + i, p))
    kvspec = pl.BlockSpec((seq, LANES), lambda b, p, i: (b, p))
    return pl.pallas_call(
        _fox_kernel,
        grid=(batch, FOX_HEADS // 2, nq),
        in_specs=[qspec, kvspec, kvspec,
                  pl.BlockSpec((tq, LANES), lambda b, p, i: (b * nq + i, 0)),
                  pl.BlockSpec((1, FOX_HEADS, seq), lambda b, p, i: (b, 0, 0)),
                  pl.BlockSpec((1, LANES), lambda b, p, i: (0, p))],
        out_specs=qspec,
        out_shape=jax.ShapeDtypeStruct((n, FOX_WIDTH), BF16),
        compiler_params=pltpu.CompilerParams(dimension_semantics=("arbitrary",) * 3, vmem_limit_bytes=VMEM_LIMIT),
        name="fox_attn",
    )(q, k, v, c, c_rows, g_fox)


def _mem_kv_kernel(m_ref, g_ref, w_ref, gk_ref, k_ref, v_ref):
    h = _rms(m_ref[...], g_ref[...]).astype(BF16)
    kv = jnp.dot(h, w_ref[...], preferred_element_type=F32)
    gk = gk_ref[...]
    ks = [_rms(kv[:, a * XA_HEAD_DIM:(a + 1) * XA_HEAD_DIM], gk) for a in range(XA_HEADS)]
    k_ref[...] = jnp.concatenate(ks, axis=1).astype(BF16)
    v_ref[...] = kv[:, XA_WIDTH:].astype(BF16)


def _mem_kv(mem, g, w_xkv, g_xk):
    n = mem.shape[0]
    tm = ROW_TILE
    row = lambda width: pl.BlockSpec((tm, width), lambda i: (i, 0))
    return pl.pallas_call(
        _mem_kv_kernel,
        grid=(n // tm,),
        in_specs=[row(D_MODEL), _const_spec((1, D_MODEL)), _const_spec(w_xkv.shape), _const_spec((1, XA_HEAD_DIM))],
        out_specs=[row(XA_WIDTH)] * 2,
        out_shape=[jax.ShapeDtypeStruct((n, XA_WIDTH), BF16)] * 2,
        compiler_params=pltpu.CompilerParams(dimension_semantics=("arbitrary",), vmem_limit_bytes=VMEM_LIMIT),
        name="mem_kv",
    )(mem, g, w_xkv, g_xk)


def _post_kernel(x_ref, yc_ref, yf_ref, wo_ref, g_ref, wq_ref, gq_ref, k_ref, v_ref, wxo_ref, o_ref):
    x1 = (x_ref[...]
          + jnp.dot(yc_ref[...], wo_ref[0:CONV_CH, :], preferred_element_type=F32)
          + jnp.dot(yf_ref[...], wo_ref[CONV_CH:, :], preferred_element_type=F32))
    hx = _rms(x1, g_ref[...]).astype(BF16)
    qx = jnp.dot(hx, wq_ref[...], preferred_element_type=F32)
    gq = gq_ref[...]
    heads = []
    for a in range(XA_HEADS):
        sl = slice(a * XA_HEAD_DIM, (a + 1) * XA_HEAD_DIM)
        qa = _rms(qx[:, sl], gq).astype(BF16)
        s = lax.dot_general(qa, k_ref[:, sl], (((1,), (1,)), ((), ())),
                            preferred_element_type=F32) * (XA_HEAD_DIM ** -0.5)
        e = jnp.exp(s - jnp.max(s, axis=1, keepdims=True))
        p = (e / jnp.sum(e, axis=1, keepdims=True)).astype(BF16)
        heads.append(jnp.dot(p, v_ref[:, sl], preferred_element_type=F32).astype(BF16))
    att = jnp.concatenate(heads, axis=1)
    o_ref[...] = x1 + jnp.dot(att, wxo_ref[...], preferred_element_type=F32)


def _post(x, yc, yf, w_out, g_xa, w_xq, g_xq, kx, vx, w_xo, seq, n_mem):
    n = x.shape[0]
    tm = ROW_TILE
    tiles_per_seq = seq // tm
    row = lambda width: pl.BlockSpec((tm, width), lambda i: (i, 0))
    memspec = pl.BlockSpec((n_mem, XA_WIDTH), lambda i: (i // tiles_per_seq, 0))
    return pl.pallas_call(
        _post_kernel,
        grid=(n // tm,),
        in_specs=[row(D_MODEL), row(CONV_CH), row(FOX_WIDTH), _const_spec(w_out.shape), _const_spec((1, D_MODEL)),
                  _const_spec(w_xq.shape), _const_spec((1, XA_HEAD_DIM)), memspec, memspec, _const_spec(w_xo.shape)],
        out_specs=row(D_MODEL),
        out_shape=jax.ShapeDtypeStruct((n, D_MODEL), F32),
        compiler_params=pltpu.CompilerParams(dimension_semantics=("arbitrary",), vmem_limit_bytes=VMEM_LIMIT),
        name="post",
    )(x, yc, yf, w_out, g_xa, w_xq, g_xq, kx, vx, w_xo)


def _swiglu_acc(h, wg_ref, wu_ref, wd_ref):
    acc = None
    for c in range(D_FF // FF_CHUNK):
        sl = slice(c * FF_CHUNK, (c + 1) * FF_CHUNK)
        gate = jnp.dot(h, wg_ref[:, sl], preferred_element_type=F32)
        up = jnp.dot(h, wu_ref[:, sl], preferred_element_type=F32)
        a = (gate * (1.0 / (1.0 + jnp.exp(-gate))) * up).astype(BF16)
        part = jnp.dot(a, wd_ref[sl, :], preferred_element_type=F32)
        acc = part if acc is None else acc + part
    return acc


def _ffn_kernel(x_ref, g_ref, wg_ref, wu_ref, wd_ref, o_ref):
    x = x_ref[...]
    h = _rms(x, g_ref[...]).astype(BF16)
    o_ref[...] = x + _swiglu_acc(h, wg_ref, wu_ref, wd_ref)


def _ffn(x, g, wg, wu, wd):
    n = x.shape[0]
    tm = ROW_TILE
    row = pl.BlockSpec((tm, D_MODEL), lambda i: (i, 0))
    return pl.pallas_call(
        _ffn_kernel,
        grid=(n // tm,),
        in_specs=[row, _const_spec((1, D_MODEL)), _const_spec(wg.shape), _const_spec(wu.shape),
                  _const_spec(wd.shape)],
        out_specs=row,
        out_shape=jax.ShapeDtypeStruct((n, D_MODEL), F32),
        compiler_params=pltpu.CompilerParams(dimension_semantics=("arbitrary",), vmem_limit_bytes=VMEM_LIMIT),
        name="ffn",
    )(x, g, wg, wu, wd)


def _moe_kernel(x_ref, g_ref, wr_hi_ref, wr_lo_ref, br_ref, wg_ref, wu_ref, wd_ref, o_ref, h_scr, gate_scr):
    e = pl.program_id(1)
    lane = lax.broadcasted_iota(jnp.int32, gate_scr.shape, 1).astype(F32)

    @pl.when(e == 0)
    def _():
        x = x_ref[...]
        hf = _rms(x, g_ref[...])
        h_hi, h_lo = _split2(hf)
        h_scr[...] = h_hi
        logits = (jnp.dot(h_hi, wr_hi_ref[...], preferred_element_type=F32)
                  + jnp.dot(h_hi, wr_lo_ref[...], preferred_element_type=F32)
                  + jnp.dot(h_lo, wr_hi_ref[...], preferred_element_type=F32)) + br_ref[...]
        logits = jnp.where(lane < N_EXPERTS, logits, -jnp.inf)
        m1 = jnp.max(logits, axis=1, keepdims=True)
        i1 = jnp.min(jnp.where(logits == m1, lane, LANES), axis=1, keepdims=True)
        rest = jnp.where(lane == i1, -jnp.inf, logits)
        m2 = jnp.max(rest, axis=1, keepdims=True)
        i2 = jnp.min(jnp.where(rest == m2, lane, LANES), axis=1, keepdims=True)
        e2 = jnp.exp(m2 - m1)
        w1 = 1.0 / (1.0 + e2)
        gate_scr[...] = jnp.where(lane == i1, w1, jnp.where(lane == i2, e2 * w1, 0.0))
        o_ref[...] = x

    gate = jnp.sum(jnp.where(lane == e.astype(F32), gate_scr[...], 0.0), axis=1, keepdims=True)
    o_ref[...] += gate * _swiglu_acc(h_scr[...], wg_ref, wu_ref, wd_ref)


def _moe(x, g, wr_hi, wr_lo, b_r, wg, wu, wd):
    n = x.shape[0]
    tm = ROW_TILE
    row = pl.BlockSpec((tm, D_MODEL), lambda i, e: (i, 0))
    return pl.pallas_call(
        _moe_kernel,
        grid=(n // tm, N_EXPERTS),
        in_specs=[row, _const_spec((1, D_MODEL)), _const_spec(wr_hi.shape), _const_spec(wr_lo.shape),
                  _const_spec((1, LANES)),
                  pl.BlockSpec((None, D_MODEL, D_FF), lambda i, e: (e, 0, 0)),
                  pl.BlockSpec((None, D_MODEL, D_FF), lambda i, e: (e, 0, 0)),
                  pl.BlockSpec((None, D_FF, D_MODEL), lambda i, e: (e, 0, 0))],
        out_specs=row,
        out_shape=jax.ShapeDtypeStruct((n, D_MODEL), F32),
        scratch_shapes=[pltpu.VMEM((tm, D_MODEL), BF16), pltpu.VMEM((tm, LANES), F32)],
        compiler_params=pltpu.CompilerParams(dimension_semantics=("arbitrary",) * 2, vmem_limit_bytes=VMEM_LIMIT),
        name="moe",
    )(x, g, wr_hi, wr_lo, b_r, wg, wu, wd)


def _pad_lanes(a, width=LANES):
    return jnp.pad(a, ((0, 0), (0, width - a.shape[1])))


def kernel(x, mem, g_mix, w_in, conv_w, b_f, g_q, g_k, g_conv_out, g_fox_out, w_out, g_xa, g_mem, w_xq, w_xkv,
           g_xq, g_xk, w_xo, g_ffn, w_gate, w_up, w_down, w_router, b_router, we_gate, we_up, we_down):
    batch, seq, _ = x.shape
    n_mem = mem.shape[1]
    xs = x.reshape(batch * seq, D_MODEL)
    mems = mem.reshape(batch * n_mem, D_MODEL)
    n_main = 3 * CONV_CH + 3 * FOX_WIDTH

    group = jnp.arange(MXU_DIM) // HEAD_DIM
    gmat = (group[:, None] == group[None, :]).astype(BF16)
    tri = jnp.tril(jnp.ones((ROW_TILE, ROW_TILE), BF16))
    row1 = lambda a: a.reshape(1, -1)

    for l in range(DEPTH):
        w_f = _pad_lanes(jnp.tile(w_in[l][:, n_main:], (1, 3))).astype(BF16)
        b_f3 = _pad_lanes(jnp.tile(row1(b_f[l]), (1, 3)))
        yc, q, k, v, c = _mix_in(xs, row1(g_mix[l]), w_in[l][:, :n_main].astype(BF16), w_f, conv_w[l], b_f3,
                                 jnp.tile(row1(g_q[l]), (1, FOX_HEADS)), jnp.tile(row1(g_k[l]), (1, FOX_HEADS)),
                                 row1(g_conv_out[l]), gmat, tri, seq)
        c_rows = c[:, :FOX_HEADS].reshape(batch, seq, FOX_HEADS).transpose(0, 2, 1)
        yf = _fox_attention(q, k, v, c, c_rows, row1(g_fox_out[l]), batch, seq)
        kx, vx = _mem_kv(mems, row1(g_mem[l]), w_xkv[l].astype(BF16), row1(g_xk[l]))
        xs = _post(xs, yc, yf, w_out[l].astype(BF16), row1(g_xa[l]), w_xq[l].astype(BF16), row1(g_xq[l]),
                   kx, vx, w_xo[l].astype(BF16), seq, n_mem)
        i = l // 2
        if l % 2 == 0:
            xs = _ffn(xs, row1(g_ffn[l]), w_gate[i].astype(BF16), w_up[i].astype(BF16), w_down[i].astype(BF16))
        else:
            wr = _pad_lanes(w_router[i])
            wr_hi = wr.astype(BF16)
            wr_lo = (wr - wr_hi.astype(F32)).astype(BF16)
            xs = _moe(xs, row1(g_ffn[l]), wr_hi, wr_lo, _pad_lanes(row1(b_router[i])),
                      we_gate[i].astype(BF16), we_up[i].astype(BF16), we_down[i].astype(BF16))
    return xs.reshape(batch, seq, D_MODEL)
```

```python
import functools

import jax
import jax.numpy as jnp
import numpy as np
from jax import lax
from jax.experimental import pallas as pl
from jax.experimental.pallas import tpu as pltpu

D_MODEL = 1024
DEPTH = 4
HEAD_DIM = 64
CONV_CH = 512
CONV_K = 3
FOX_HEADS = 8
FOX_WIDTH = 512
XA_HEADS = 4
XA_HEAD_DIM = 128
XA_WIDTH = 512
D_FF = 2816
N_EXPERTS = 8
EPS = 1e-6

LANES = 128
SUBLANES = 8
MXU_DIM = 256
VMEM_LIMIT = 56 * 1024 * 1024

ROW_TILE = 512
ATT_TILE = 512
FF_CHUNK = 256

F32 = jnp.float32
BF16 = jnp.bfloat16


def _const_spec(shape):
    return pl.BlockSpec(shape, lambda *_: (0,) * len(shape), pipeline_mode=pl.Buffered(1))


def _rms(x, g):
    return x * lax.rsqrt(jnp.mean(x * x, axis=-1, keepdims=True) + EPS) * g


def _split2(x):
    hi = x.astype(BF16)
    lo = (x - hi.astype(F32)).astype(BF16)
    return hi, lo


def _group_rms(y, g, gmat):
    hi, lo = _split2(y * y)
    parts = []
    for c in range(y.shape[1] // MXU_DIM):
        sl = slice(c * MXU_DIM, (c + 1) * MXU_DIM)
        parts.append(jnp.dot(hi[:, sl], gmat, preferred_element_type=F32)
                     + jnp.dot(lo[:, sl], gmat, preferred_element_type=F32))
    ms = jnp.concatenate(parts, axis=1) * (1.0 / HEAD_DIM)
    return y * lax.rsqrt(ms + EPS) * g


def _mix_in_kernel(tiles_per_seq, x_ref, g_ref, w_ref, wf_ref, cw_ref, bf_ref, gq_ref, gk_ref, gc_ref,
                   gmat_ref, tri_ref, place_ref, yc_ref, q_ref, k0_ref, k1_ref, v0_ref, v1_ref, ubuf, ccarry):
    i = pl.program_id(0)
    tm = x_ref.shape[0]
    h = _rms(x_ref[...], g_ref[...]).astype(BF16)

    def proj(j):
        return jnp.dot(h, w_ref[:, j * CONV_CH:(j + 1) * CONV_CH], preferred_element_type=F32)

    @pl.when(i % tiles_per_seq == 0)
    def _():
        ubuf[0:SUBLANES, :] = jnp.zeros((SUBLANES, CONV_CH), F32)
        ccarry[...] = jnp.zeros_like(ccarry)

    ubuf[SUBLANES:SUBLANES + tm, :] = proj(1) * proj(2)
    y = (cw_ref[0:1, :] * ubuf[SUBLANES - 2:SUBLANES - 2 + tm, :]
         + cw_ref[1:2, :] * ubuf[SUBLANES - 1:SUBLANES - 1 + tm, :]
         + cw_ref[2:3, :] * ubuf[SUBLANES:SUBLANES + tm, :])
    ubuf[0:SUBLANES, :] = ubuf[tm:tm + SUBLANES, :]
    gmat = gmat_ref[...]
    yc_ref[...] = _group_rms(proj(0) * y, gc_ref[...], gmat).astype(BF16)

    q_ref[...] = (_group_rms(proj(3), gq_ref[...], gmat) * (HEAD_DIM ** -0.5)).astype(BF16)

    z = jnp.dot(h, wf_ref[...], preferred_element_type=F32) + bf_ref[...]
    logf = jnp.minimum(z, 0.0) - jnp.log1p(jnp.exp(-jnp.abs(z)))
    p1 = logf.astype(BF16)
    r1 = logf - p1.astype(F32)
    p2 = r1.astype(BF16)
    p3 = (r1 - p2.astype(F32)).astype(BF16)
    lane = lax.broadcasted_iota(jnp.int32, logf.shape, 1)
    parts = jnp.where(lane < FOX_HEADS, p1, jnp.where(lane < 2 * FOX_HEADS, p2, p3))
    cs = jnp.dot(tri_ref[...], parts, preferred_element_type=F32)
    cs = cs + pltpu.roll(cs, LANES - FOX_HEADS, axis=1) + pltpu.roll(cs, LANES - 2 * FOX_HEADS, axis=1)
    c = jnp.where(lane < FOX_HEADS, cs + ccarry[...], 0.0)
    ccarry[...] = c[tm - 1:tm, :]

    c3 = -(c + pltpu.roll(c, FOX_HEADS, axis=1) + pltpu.roll(c, 2 * FOX_HEADS, axis=1))
    n1 = c3.astype(BF16)
    r1 = c3 - n1.astype(F32)
    n2 = r1.astype(BF16)
    n3 = (r1 - n2.astype(F32)).astype(BF16)
    nparts = jnp.where(lane < FOX_HEADS, n1, jnp.where(lane < 2 * FOX_HEADS, n2, n3))
    kn = _group_rms(proj(4), gk_ref[...], gmat)
    vv = proj(5)
    lane_w = lax.broadcasted_iota(jnp.int32, (1, FOX_WIDTH), 1)
    for j, (k_ref, v_ref) in enumerate(((k0_ref, v0_ref), (k1_ref, v1_ref))):
        own = (lane_w & HEAD_DIM) == j * HEAD_DIM
        aug = jnp.dot(nparts, place_ref[j], preferred_element_type=F32)
        k_ref[...] = jnp.where(own, kn, aug).astype(BF16)
        ones_lane = (lane_w & (LANES - 1)) == (1 - j) * HEAD_DIM
        v_ref[...] = jnp.where(own, vv, jnp.where(ones_lane, 1.0, 0.0)).astype(BF16)


def _aug_placement():
    place = np.zeros((2, LANES, FOX_WIDTH), np.float32)
    for head in range(FOX_HEADS):
        pair, j = divmod(head, 2)
        for m in range(3):
            place[j, FOX_HEADS * m + head, pair * LANES + (1 - j) * HEAD_DIM + m] = 1.0
    return jnp.asarray(place, BF16)


def _mix_in(x, g, w_main, w_f, conv_w, b_f, gq, gk, gc, gmat, tri, place, seq):
    n = x.shape[0]
    tm = ROW_TILE
    row = lambda width: pl.BlockSpec((tm, width), lambda i: (i, 0))
    return pl.pallas_call(
        functools.partial(_mix_in_kernel, seq // tm),
        grid=(n // tm,),
        in_specs=[row(D_MODEL), _const_spec((1, D_MODEL)), _const_spec(w_main.shape), _const_spec(w_f.shape),
                  _const_spec(conv_w.shape), _const_spec((1, LANES)), _const_spec((1, CONV_CH)),
                  _const_spec((1, CONV_CH)), _const_spec((1, CONV_CH)), _const_spec(gmat.shape),
                  _const_spec(tri.shape), _const_spec(place.shape)],
        out_specs=[row(CONV_CH)] * 6,
        out_shape=[jax.ShapeDtypeStruct((n, CONV_CH), BF16)] * 6,
        scratch_shapes=[pltpu.VMEM((tm + SUBLANES, CONV_CH), F32), pltpu.VMEM((1, LANES), F32)],
        compiler_params=pltpu.CompilerParams(dimension_semantics=("arbitrary",), vmem_limit_bytes=VMEM_LIMIT),
        name="mix_in",
    )(x, g, w_main, w_f, conv_w, b_f, gq, gk, gc, gmat, tri, place)


def _fox_kernel(q_ref, k0_ref, k1_ref, v0_ref, v1_ref, g_ref, o_ref, sa_scr, sb_scr):
    qi = pl.program_id(2)
    tq = q_ref.shape[0]
    tk = ATT_TILE
    q = q_ref[...]
    k_refs = (k0_ref, k1_ref)
    v_refs = (v0_ref, v1_ref)
    lane = lax.broadcasted_iota(jnp.int32, (1, LANES), 1)
    own = [(lane & HEAD_DIM) == j * HEAD_DIM for j in range(2)]
    spare = [(1 - j) * HEAD_DIM for j in range(2)]
    qa = [jnp.where(own[j], q, jnp.where((lane >= spare[j]) & (lane < spare[j] + 3), 1.0, 0.0).astype(BF16))
          for j in range(2)]
    causal = lax.broadcasted_iota(jnp.int32, (tq, tk), 0) >= lax.broadcasted_iota(jnp.int32, (tq, tk), 1)

    def scores(kv, s_scr):
        start = pl.multiple_of(kv * tk, tk)
        for j in range(2):
            s_scr[j] = lax.dot_general(qa[j], k_refs[j][pl.ds(start, tk), :], (((1,), (1,)), ((), ())),
                                       preferred_element_type=F32)

    def absorb(kv, s_scr, carry, on_diagonal=False):
        start = pl.multiple_of(kv * tk, tk)
        new = []
        for j in range(2):
            m, acc = carry[j]
            s = s_scr[j]
            if on_diagonal:
                s = jnp.where(causal, s, -jnp.inf)
            m_new = jnp.maximum(m, jnp.max(s, axis=1, keepdims=True))
            p = jnp.exp(s - m_new).astype(BF16)
            acc = jnp.exp(m - m_new) * acc + jnp.dot(p, v_refs[j][pl.ds(start, tk), :], preferred_element_type=F32)
            new.append((m_new, acc))
        return tuple(new)

    def two_blocks(t, carry):
        kv = 2 * t
        scores(kv + 1, sb_scr)
        carry = absorb(kv, sa_scr, carry)
        scores(kv + 2, sa_scr)
        return absorb(kv + 1, sb_scr, carry)

    def odd_tail(carry):
        scores(qi, sb_scr)
        carry = absorb(qi - 1, sa_scr, carry)
        return absorb(qi, sb_scr, carry, on_diagonal=True)

    init = tuple((jnp.full((tq, 1), -jnp.inf, F32), jnp.zeros((tq, LANES), F32)) for _ in range(2))
    scores(0, sa_scr)
    carry = lax.fori_loop(0, qi // 2, two_blocks, init)
    carry = lax.cond(qi % 2 == 1, odd_tail, lambda c: absorb(qi, sa_scr, c, on_diagonal=True), carry)
    out = None
    for j in range(2):
        acc = carry[j][1]
        denom = jnp.sum(jnp.where(lane == spare[j], acc, 0.0), axis=1, keepdims=True)
        o = jnp.where(own[j], acc / denom, 0.0)
        ms = jnp.sum(o * o, axis=1, keepdims=True) * (1.0 / HEAD_DIM)
        o = o * lax.rsqrt(ms + EPS)
        out = o if out is None else out + o
    o_ref[...] = (out * g_ref[...]).astype(BF16)


def _fox_attention(q, k0, k1, v0, v1, g_fox, batch, seq):
    n = q.shape[0]
    tq = ATT_TILE
    nq = seq // tq
    qspec = pl.BlockSpec((tq, LANES), lambda b, p, i: (b * nq + i, p))
    kvspec = pl.BlockSpec((seq, LANES), lambda b, p, i: (b, p))
    return pl.pallas_call(
        _fox_kernel,
        grid=(batch, FOX_HEADS // 2, nq),
        in_specs=[qspec, kvspec, kvspec, kvspec, kvspec, pl.BlockSpec((1, LANES), lambda b, p, i: (0, p))],
        out_specs=qspec,
        out_shape=jax.ShapeDtypeStruct((n, FOX_WIDTH), BF16),
        scratch_shapes=[pltpu.VMEM((2, tq, ATT_TILE), F32)] * 2,
        compiler_params=pltpu.CompilerParams(dimension_semantics=("arbitrary",) * 3, vmem_limit_bytes=VMEM_LIMIT),
        name="fox_attn",
    )(q, k0, k1, v0, v1, g_fox)


def _mem_kv_kernel(m_ref, g_ref, w_ref, gk_ref, k_ref, v_ref):
    h = _rms(m_ref[...], g_ref[...]).astype(BF16)
    kv = jnp.dot(h, w_ref[...], preferred_element_type=F32)
    gk = gk_ref[...]
    ks = [_rms(kv[:, a * XA_HEAD_DIM:(a + 1) * XA_HEAD_DIM], gk) for a in range(XA_HEADS)]
    k_ref[...] = jnp.concatenate(ks, axis=1).astype(BF16)
    v_ref[...] = kv[:, XA_WIDTH:].astype(BF16)


def _mem_kv(mem, g, w_xkv, g_xk):
    n = mem.shape[0]
    tm = ROW_TILE
    row = lambda width: pl.BlockSpec((tm, width), lambda i: (i, 0))
    return pl.pallas_call(
        _mem_kv_kernel,
        grid=(n // tm,),
        in_specs=[row(D_MODEL), _const_spec((1, D_MODEL)), _const_spec(w_xkv.shape), _const_spec((1, XA_HEAD_DIM))],
        out_specs=[row(XA_WIDTH)] * 2,
        out_shape=[jax.ShapeDtypeStruct((n, XA_WIDTH), BF16)] * 2,
        compiler_params=pltpu.CompilerParams(dimension_semantics=("arbitrary",), vmem_limit_bytes=VMEM_LIMIT),
        name="mem_kv",
    )(mem, g, w_xkv, g_xk)


def _post_kernel(x_ref, yc_ref, yf_ref, wo_ref, g_ref, wq_ref, gq_ref, k_ref, v_ref, wxo_ref, o_ref):
    x1 = (x_ref[...]
          + jnp.dot(yc_ref[...], wo_ref[0:CONV_CH, :], preferred_element_type=F32)
          + jnp.dot(yf_ref[...], wo_ref[CONV_CH:, :], preferred_element_type=F32))
    hx = _rms(x1, g_ref[...]).astype(BF16)
    qx = jnp.dot(hx, wq_ref[...], preferred_element_type=F32)
    gq = gq_ref[...]
    heads = []
    for a in range(XA_HEADS):
        sl = slice(a * XA_HEAD_DIM, (a + 1) * XA_HEAD_DIM)
        qa = _rms(qx[:, sl], gq).astype(BF16)
        s = lax.dot_general(qa, k_ref[:, sl], (((1,), (1,)), ((), ())),
                            preferred_element_type=F32) * (XA_HEAD_DIM ** -0.5)
        e = jnp.exp(s - jnp.max(s, axis=1, keepdims=True))
        p = (e / jnp.sum(e, axis=1, keepdims=True)).astype(BF16)
        heads.append(jnp.dot(p, v_ref[:, sl], preferred_element_type=F32).astype(BF16))
    att = jnp.concatenate(heads, axis=1)
    o_ref[...] = x1 + jnp.dot(att, wxo_ref[...], preferred_element_type=F32)


def _post(x, yc, yf, w_out, g_xa, w_xq, g_xq, kx, vx, w_xo, seq, n_mem):
    n = x.shape[0]
    tm = ROW_TILE
    tiles_per_seq = seq // tm
    row = lambda width: pl.BlockSpec((tm, width), lambda i: (i, 0))
    memspec = pl.BlockSpec((n_mem, XA_WIDTH), lambda i: (i // tiles_per_seq, 0))
    return pl.pallas_call(
        _post_kernel,
        grid=(n // tm,),
        in_specs=[row(D_MODEL), row(CONV_CH), row(FOX_WIDTH), _const_spec(w_out.shape), _const_spec((1, D_MODEL)),
                  _const_spec(w_xq.shape), _const_spec((1, XA_HEAD_DIM)), memspec, memspec, _const_spec(w_xo.shape)],
        out_specs=row(D_MODEL),
        out_shape=jax.ShapeDtypeStruct((n, D_MODEL), F32),
        compiler_params=pltpu.CompilerParams(dimension_semantics=("arbitrary",), vmem_limit_bytes=VMEM_LIMIT),
        name="post",
    )(x, yc, yf, w_out, g_xa, w_xq, g_xq, kx, vx, w_xo)


def _swiglu_acc(h, wg_ref, wu_ref, wd_ref):
    acc = None
    for c in range(D_FF // FF_CHUNK):
        sl = slice(c * FF_CHUNK, (c + 1) * FF_CHUNK)
        gate = jnp.dot(h, wg_ref[:, sl], preferred_element_type=F32)
        up = jnp.dot(h, wu_ref[:, sl], preferred_element_type=F32)
        a = (gate * (1.0 / (1.0 + jnp.exp(-gate))) * up).astype(BF16)
        part = jnp.dot(a, wd_ref[sl, :], preferred_element_type=F32)
        acc = part if acc is None else acc + part
    return acc


def _ffn_kernel(x_ref, g_ref, wg_ref, wu_ref, wd_ref, o_ref):
    x = x_ref[...]
    h = _rms(x, g_ref[...]).astype(BF16)
    o_ref[...] = x + _swiglu_acc(h, wg_ref, wu_ref, wd_ref)


def _ffn(x, g, wg, wu, wd):
    n = x.shape[0]
    tm = ROW_TILE
    row = pl.BlockSpec((tm, D_MODEL), lambda i: (i, 0))
    return pl.pallas_call(
        _ffn_kernel,
        grid=(n // tm,),
        in_specs=[row, _const_spec((1, D_MODEL)), _const_spec(wg.shape), _const_spec(wu.shape),
                  _const_spec(wd.shape)],
        out_specs=row,
        out_shape=jax.ShapeDtypeStruct((n, D_MODEL), F32),
        compiler_params=pltpu.CompilerParams(dimension_semantics=("arbitrary",), vmem_limit_bytes=VMEM_LIMIT),
        name="ffn",
    )(x, g, wg, wu, wd)


def _moe_kernel(x_ref, g_ref, wr_hi_ref, wr_lo_ref, br_ref, wg_ref, wu_ref, wd_ref, o_ref, h_scr, gate_scr):
    e = pl.program_id(1)
    lane = lax.broadcasted_iota(jnp.int32, gate_scr.shape, 1).astype(F32)

    @pl.when(e == 0)
    def _():
        x = x_ref[...]
        hf = _rms(x, g_ref[...])
        h_hi, h_lo = _split2(hf)
        h_scr[...] = h_hi
        logits = (jnp.dot(h_hi, wr_hi_ref[...], preferred_element_type=F32)
                  + jnp.dot(h_hi, wr_lo_ref[...], preferred_element_type=F32)
                  + jnp.dot(h_lo, wr_hi_ref[...], preferred_element_type=F32)) + br_ref[...]
        logits = jnp.where(lane < N_EXPERTS, logits, -jnp.inf)
        m1 = jnp.max(logits, axis=1, keepdims=True)
        i1 = jnp.min(jnp.where(logits == m1, lane, LANES), axis=1, keepdims=True)
        rest = jnp.where(lane == i1, -jnp.inf, logits)
        m2 = jnp.max(rest, axis=1, keepdims=True)
        i2 = jnp.min(jnp.where(rest == m2, lane, LANES), axis=1, keepdims=True)
        e2 = jnp.exp(m2 - m1)
        w1 = 1.0 / (1.0 + e2)
        gate_scr[...] = jnp.where(lane == i1, w1, jnp.where(lane == i2, e2 * w1, 0.0))
        o_ref[...] = x

    gate = jnp.sum(jnp.where(lane == e.astype(F32), gate_scr[...], 0.0), axis=1, keepdims=True)
    o_ref[...] += gate * _swiglu_acc(h_scr[...], wg_ref, wu_ref, wd_ref)


def _moe(x, g, wr_hi, wr_lo, b_r, wg, wu, wd):
    n = x.shape[0]
    tm = ROW_TILE
    row = pl.BlockSpec((tm, D_MODEL), lambda i, e: (i, 0))
    return pl.pallas_call(
        _moe_kernel,
        grid=(n // tm, N_EXPERTS),
        in_specs=[row, _const_spec((1, D_MODEL)), _const_spec(wr_hi.shape), _const_spec(wr_lo.shape),
                  _const_spec((1, LANES)),
                  pl.BlockSpec((None, D_MODEL, D_FF), lambda i, e: (e, 0, 0)),
                  pl.BlockSpec((None, D_MODEL, D_FF), lambda i, e: (e, 0, 0)),
                  pl.BlockSpec((None, D_FF, D_MODEL), lambda i, e: (e, 0, 0))],
        out_specs=row,
        out_shape=jax.ShapeDtypeStruct((n, D_MODEL), F32),
        scratch_shapes=[pltpu.VMEM((tm, D_MODEL), BF16), pltpu.VMEM((tm, LANES), F32)],
        compiler_params=pltpu.CompilerParams(dimension_semantics=("arbitrary",) * 2, vmem_limit_bytes=VMEM_LIMIT),
        name="moe",
    )(x, g, wr_hi, wr_lo, b_r, wg, wu, wd)


def _pad_lanes(a, width=LANES):
    return jnp.pad(a, ((0, 0), (0, width - a.shape[1])))


def kernel(x, mem, g_mix, w_in, conv_w, b_f, g_q, g_k, g_conv_out, g_fox_out, w_out, g_xa, g_mem, w_xq, w_xkv,
           g_xq, g_xk, w_xo, g_ffn, w_gate, w_up, w_down, w_router, b_router, we_gate, we_up, we_down):
    batch, seq, _ = x.shape
    n_mem = mem.shape[1]
    xs = x.reshape(batch * seq, D_MODEL)
    mems = mem.reshape(batch * n_mem, D_MODEL)
    n_main = 3 * CONV_CH + 3 * FOX_WIDTH

    group = jnp.arange(MXU_DIM) // HEAD_DIM
    gmat = (group[:, None] == group[None, :]).astype(BF16)
    tri = jnp.tril(jnp.ones((ROW_TILE, ROW_TILE), BF16))
    place = _aug_placement()
    row1 = lambda a: a.reshape(1, -1)

    for l in range(DEPTH):
        w_f = _pad_lanes(jnp.tile(w_in[l][:, n_main:], (1, 3))).astype(BF16)
        b_f3 = _pad_lanes(jnp.tile(row1(b_f[l]), (1, 3)))
        yc, q, k0, k1, v0, v1 = _mix_in(
            xs, row1(g_mix[l]), w_in[l][:, :n_main].astype(BF16), w_f, conv_w[l], b_f3,
            jnp.tile(row1(g_q[l]), (1, FOX_HEADS)), jnp.tile(row1(g_k[l]), (1, FOX_HEADS)),
            row1(g_conv_out[l]), gmat, tri, place, seq)
        yf = _fox_attention(q, k0, k1, v0, v1, row1(g_fox_out[l]), batch, seq)
        kx, vx = _mem_kv(mems, row1(g_mem[l]), w_xkv[l].astype(BF16), row1(g_xk[l]))
        xs = _post(xs, yc, yf, w_out[l].astype(BF16), row1(g_xa[l]), w_xq[l].astype(BF16), row1(g_xq[l]),
                   kx, vx, w_xo[l].astype(BF16), seq, n_mem)
        i = l // 2
        if l % 2 == 0:
            xs = _ffn(xs, row1(g_ffn[l]), w_gate[i].astype(BF16), w_up[i].astype(BF16), w_down[i].astype(BF16))
        else:
            wr = _pad_lanes(w_router[i])
            wr_hi = wr.astype(BF16)
            wr_lo = (wr - wr_hi.astype(F32)).astype(BF16)
            xs = _moe(xs, row1(g_ffn[l]), wr_hi, wr_lo, _pad_lanes(row1(b_router[i])),
                      we_gate[i].astype(BF16), we_up[i].astype(BF16), we_down[i].astype(BF16))
    return xs.reshape(batch, seq, D_MODEL)
```

```python
import functools

import jax
import jax.numpy as jnp
import numpy as np
from jax import lax
from jax.experimental import pallas as pl
from jax.experimental.pallas import tpu as pltpu

D_MODEL = 1024
DEPTH = 4
HEAD_DIM = 64
CONV_CH = 512
CONV_K = 3
FOX_HEADS = 8
FOX_WIDTH = 512
XA_HEADS = 4
XA_HEAD_DIM = 128
XA_WIDTH = 512
D_FF = 2816
N_EXPERTS = 8
EPS = 1e-6

LANES = 128
SUBLANES = 8
MXU_DIM = 256
VMEM_LIMIT = 56 * 1024 * 1024

ROW_TILE = 512
ATT_TILE = 512
FF_CHUNK = 256
EXP_TILE = 512
SEG_ALIGN = SUBLANES
SEG_BITS = (ROW_TILE // SEG_ALIGN).bit_length()
PAD_BITS = (EXP_TILE // SEG_ALIGN - 1).bit_length()
LOCAL_SLOTS = -(-(2 * ROW_TILE + N_EXPERTS * (SEG_ALIGN - 1)) // LANES) * LANES
META_EXPERT, META_RANK, META_GATE = 0, 4, (8, 16)

F32 = jnp.float32
BF16 = jnp.bfloat16


def _const_spec(shape):
    return pl.BlockSpec(shape, lambda *_: (0,) * len(shape), pipeline_mode=pl.Buffered(1))


def _rms(x, g):
    return x * lax.rsqrt(jnp.mean(x * x, axis=-1, keepdims=True) + EPS) * g


def _split2(x):
    hi = x.astype(BF16)
    lo = (x - hi.astype(F32)).astype(BF16)
    return hi, lo


def _group_rms(y, g, gmat):
    hi, lo = _split2(y * y)
    parts = []
    for c in range(y.shape[1] // MXU_DIM):
        sl = slice(c * MXU_DIM, (c + 1) * MXU_DIM)
        parts.append(jnp.dot(hi[:, sl], gmat, preferred_element_type=F32)
                     + jnp.dot(lo[:, sl], gmat, preferred_element_type=F32))
    ms = jnp.concatenate(parts, axis=1) * (1.0 / HEAD_DIM)
    return y * lax.rsqrt(ms + EPS) * g


def _mix_in_kernel(tiles_per_seq, x_ref, g_ref, w_ref, wf_ref, cw_ref, bf_ref, gq_ref, gk_ref, gc_ref,
                   gmat_ref, tri_ref, place_ref, yc_ref, q_ref, k0_ref, k1_ref, v0_ref, v1_ref, ubuf, ccarry):
    i = pl.program_id(0)
    tm = x_ref.shape[0]
    h = _rms(x_ref[...], g_ref[...]).astype(BF16)

    def proj(j):
        return jnp.dot(h, w_ref[:, j * CONV_CH:(j + 1) * CONV_CH], preferred_element_type=F32)

    @pl.when(i % tiles_per_seq == 0)
    def _():
        ubuf[0:SUBLANES, :] = jnp.zeros((SUBLANES, CONV_CH), F32)
        ccarry[...] = jnp.zeros_like(ccarry)

    ubuf[SUBLANES:SUBLANES + tm, :] = proj(1) * proj(2)
    y = (cw_ref[0:1, :] * ubuf[SUBLANES - 2:SUBLANES - 2 + tm, :]
         + cw_ref[1:2, :] * ubuf[SUBLANES - 1:SUBLANES - 1 + tm, :]
         + cw_ref[2:3, :] * ubuf[SUBLANES:SUBLANES + tm, :])
    ubuf[0:SUBLANES, :] = ubuf[tm:tm + SUBLANES, :]
    gmat = gmat_ref[...]
    yc_ref[...] = _group_rms(proj(0) * y, gc_ref[...], gmat).astype(BF16)

    q_ref[...] = (_group_rms(proj(3), gq_ref[...], gmat) * (HEAD_DIM ** -0.5)).astype(BF16)

    z = jnp.dot(h, wf_ref[...], preferred_element_type=F32) + bf_ref[...]
    logf = jnp.minimum(z, 0.0) - jnp.log1p(jnp.exp(-jnp.abs(z)))
    p1 = logf.astype(BF16)
    r1 = logf - p1.astype(F32)
    p2 = r1.astype(BF16)
    p3 = (r1 - p2.astype(F32)).astype(BF16)
    lane = lax.broadcasted_iota(jnp.int32, logf.shape, 1)
    parts = jnp.where(lane < FOX_HEADS, p1, jnp.where(lane < 2 * FOX_HEADS, p2, p3))
    cs = jnp.dot(tri_ref[...], parts, preferred_element_type=F32)
    cs = cs + pltpu.roll(cs, LANES - FOX_HEADS, axis=1) + pltpu.roll(cs, LANES - 2 * FOX_HEADS, axis=1)
    c = jnp.where(lane < FOX_HEADS, cs + ccarry[...], 0.0)
    ccarry[...] = c[tm - 1:tm, :]

    c3 = -(c + pltpu.roll(c, FOX_HEADS, axis=1) + pltpu.roll(c, 2 * FOX_HEADS, axis=1))
    n1 = c3.astype(BF16)
    r1 = c3 - n1.astype(F32)
    n2 = r1.astype(BF16)
    n3 = (r1 - n2.astype(F32)).astype(BF16)
    nparts = jnp.where(lane < FOX_HEADS, n1, jnp.where(lane < 2 * FOX_HEADS, n2, n3))
    kn = _group_rms(proj(4), gk_ref[...], gmat)
    vv = proj(5)
    lane_w = lax.broadcasted_iota(jnp.int32, (1, FOX_WIDTH), 1)
    for j, (k_ref, v_ref) in enumerate(((k0_ref, v0_ref), (k1_ref, v1_ref))):
        own = (lane_w & HEAD_DIM) == j * HEAD_DIM
        aug = jnp.dot(nparts, place_ref[j], preferred_element_type=F32)
        k_ref[...] = jnp.where(own, kn, aug).astype(BF16)
        ones_lane = (lane_w & (LANES - 1)) == (1 - j) * HEAD_DIM
        v_ref[...] = jnp.where(own, vv, jnp.where(ones_lane, 1.0, 0.0)).astype(BF16)


def _aug_placement():
    place = np.zeros((2, LANES, FOX_WIDTH), np.float32)
    for head in range(FOX_HEADS):
        pair, j = divmod(head, 2)
        for m in range(3):
            place[j, FOX_HEADS * m + head, pair * LANES + (1 - j) * HEAD_DIM + m] = 1.0
    return jnp.asarray(place, BF16)


def _mix_in(x, g, w_main, w_f, conv_w, b_f, gq, gk, gc, gmat, tri, place, seq):
    n = x.shape[0]
    tm = ROW_TILE
    row = lambda width: pl.BlockSpec((tm, width), lambda i: (i, 0))
    return pl.pallas_call(
        functools.partial(_mix_in_kernel, seq // tm),
        grid=(n // tm,),
        in_specs=[row(D_MODEL), _const_spec((1, D_MODEL)), _const_spec(w_main.shape), _const_spec(w_f.shape),
                  _const_spec(conv_w.shape), _const_spec((1, LANES)), _const_spec((1, CONV_CH)),
                  _const_spec((1, CONV_CH)), _const_spec((1, CONV_CH)), _const_spec(gmat.shape),
                  _const_spec(tri.shape), _const_spec(place.shape)],
        out_specs=[row(CONV_CH)] * 6,
        out_shape=[jax.ShapeDtypeStruct((n, CONV_CH), BF16)] * 6,
        scratch_shapes=[pltpu.VMEM((tm + SUBLANES, CONV_CH), F32), pltpu.VMEM((1, LANES), F32)],
        compiler_params=pltpu.CompilerParams(dimension_semantics=("arbitrary",), vmem_limit_bytes=VMEM_LIMIT),
        name="mix_in",
    )(x, g, w_main, w_f, conv_w, b_f, gq, gk, gc, gmat, tri, place)


def _fox_kernel(q_ref, k0_ref, k1_ref, v0_ref, v1_ref, g_ref, o_ref, sa_scr, sb_scr):
    qi = pl.program_id(2)
    tq = q_ref.shape[0]
    tk = ATT_TILE
    q = q_ref[...]
    k_refs = (k0_ref, k1_ref)
    v_refs = (v0_ref, v1_ref)
    lane = lax.broadcasted_iota(jnp.int32, (1, LANES), 1)
    own = [(lane & HEAD_DIM) == j * HEAD_DIM for j in range(2)]
    spare = [(1 - j) * HEAD_DIM for j in range(2)]
    qa = [jnp.where(own[j], q, jnp.where((lane >= spare[j]) & (lane < spare[j] + 3), 1.0, 0.0).astype(BF16))
          for j in range(2)]
    causal = lax.broadcasted_iota(jnp.int32, (tq, tk), 0) >= lax.broadcasted_iota(jnp.int32, (tq, tk), 1)

    def scores(kv, s_scr):
        start = pl.multiple_of(kv * tk, tk)
        for j in range(2):
            s_scr[j] = lax.dot_general(qa[j], k_refs[j][pl.ds(start, tk), :], (((1,), (1,)), ((), ())),
                                       preferred_element_type=F32)

    def absorb(kv, s_scr, carry, on_diagonal=False):
        start = pl.multiple_of(kv * tk, tk)
        new = []
        for j in range(2):
            m, acc = carry[j]
            s = s_scr[j]
            if on_diagonal:
                s = jnp.where(causal, s, -jnp.inf)
            m_new = jnp.maximum(m, jnp.max(s, axis=1, keepdims=True))
            p = jnp.exp(s - m_new).astype(BF16)
            acc = jnp.exp(m - m_new) * acc + jnp.dot(p, v_refs[j][pl.ds(start, tk), :], preferred_element_type=F32)
            new.append((m_new, acc))
        return tuple(new)

    def two_blocks(t, carry):
        kv = 2 * t
        scores(kv + 1, sb_scr)
        carry = absorb(kv, sa_scr, carry)
        scores(kv + 2, sa_scr)
        return absorb(kv + 1, sb_scr, carry)

    def odd_tail(carry):
        scores(qi, sb_scr)
        carry = absorb(qi - 1, sa_scr, carry)
        return absorb(qi, sb_scr, carry, on_diagonal=True)

    init = tuple((jnp.full((tq, 1), -jnp.inf, F32), jnp.zeros((tq, LANES), F32)) for _ in range(2))
    scores(0, sa_scr)
    carry = lax.fori_loop(0, qi // 2, two_blocks, init)
    carry = lax.cond(qi % 2 == 1, odd_tail, lambda c: absorb(qi, sa_scr, c, on_diagonal=True), carry)
    out = None
    for j in range(2):
        acc = carry[j][1]
        denom = jnp.sum(jnp.where(lane == spare[j], acc, 0.0), axis=1, keepdims=True)
        o = jnp.where(own[j], acc / denom, 0.0)
        ms = jnp.sum(o * o, axis=1, keepdims=True) * (1.0 / HEAD_DIM)
        o = o * lax.rsqrt(ms + EPS)
        out = o if out is None else out + o
    o_ref[...] = (out * g_ref[...]).astype(BF16)


def _fox_attention(q, k0, k1, v0, v1, g_fox, batch, seq):
    n = q.shape[0]
    tq = ATT_TILE
    nq = seq // tq
    qspec = pl.BlockSpec((tq, LANES), lambda b, p, i: (b * nq + i, p))
    kvspec = pl.BlockSpec((seq, LANES), lambda b, p, i: (b, p))
    return pl.pallas_call(
        _fox_kernel,
        grid=(batch, FOX_HEADS // 2, nq),
        in_specs=[qspec, kvspec, kvspec, kvspec, kvspec, pl.BlockSpec((1, LANES), lambda b, p, i: (0, p))],
        out_specs=qspec,
        out_shape=jax.ShapeDtypeStruct((n, FOX_WIDTH), BF16),
        scratch_shapes=[pltpu.VMEM((2, tq, ATT_TILE), F32)] * 2,
        compiler_params=pltpu.CompilerParams(dimension_semantics=("arbitrary",) * 3, vmem_limit_bytes=VMEM_LIMIT),
        name="fox_attn",
    )(q, k0, k1, v0, v1, g_fox)


def _mem_kv_kernel(m_ref, g_ref, w_ref, gk_ref, k_ref, v_ref):
    h = _rms(m_ref[...], g_ref[...]).astype(BF16)
    kv = jnp.dot(h, w_ref[...], preferred_element_type=F32)
    gk = gk_ref[...]
    ks = [_rms(kv[:, a * XA_HEAD_DIM:(a + 1) * XA_HEAD_DIM], gk) for a in range(XA_HEADS)]
    k_ref[...] = jnp.concatenate(ks, axis=1).astype(BF16)
    v_ref[...] = kv[:, XA_WIDTH:].astype(BF16)


def _mem_kv(mem, g, w_xkv, g_xk):
    n = mem.shape[0]
    tm = ROW_TILE
    row = lambda width: pl.BlockSpec((tm, width), lambda i: (i, 0))
    return pl.pallas_call(
        _mem_kv_kernel,
        grid=(n // tm,),
        in_specs=[row(D_MODEL), _const_spec((1, D_MODEL)), _const_spec(w_xkv.shape), _const_spec((1, XA_HEAD_DIM))],
        out_specs=[row(XA_WIDTH)] * 2,
        out_shape=[jax.ShapeDtypeStruct((n, XA_WIDTH), BF16)] * 2,
        compiler_params=pltpu.CompilerParams(dimension_semantics=("arbitrary",), vmem_limit_bytes=VMEM_LIMIT),
        name="mem_kv",
    )(mem, g, w_xkv, g_xk)


def _post_kernel(x_ref, yc_ref, yf_ref, wo_ref, g_ref, wq_ref, gq_ref, k_ref, v_ref, wxo_ref, o_ref):
    x1 = (x_ref[...]
          + jnp.dot(yc_ref[...], wo_ref[0:CONV_CH, :], preferred_element_type=F32)
          + jnp.dot(yf_ref[...], wo_ref[CONV_CH:, :], preferred_element_type=F32))
    hx = _rms(x1, g_ref[...]).astype(BF16)
    qx = jnp.dot(hx, wq_ref[...], preferred_element_type=F32)
    gq = gq_ref[...]
    heads = []
    for a in range(XA_HEADS):
        sl = slice(a * XA_HEAD_DIM, (a + 1) * XA_HEAD_DIM)
        qa = _rms(qx[:, sl], gq).astype(BF16)
        s = lax.dot_general(qa, k_ref[:, sl], (((1,), (1,)), ((), ())),
                            preferred_element_type=F32) * (XA_HEAD_DIM ** -0.5)
        e = jnp.exp(s - jnp.max(s, axis=1, keepdims=True))
        p = (e / jnp.sum(e, axis=1, keepdims=True)).astype(BF16)
        heads.append(jnp.dot(p, v_ref[:, sl], preferred_element_type=F32).astype(BF16))
    att = jnp.concatenate(heads, axis=1)
    o_ref[...] = x1 + jnp.dot(att, wxo_ref[...], preferred_element_type=F32)


def _post(x, yc, yf, w_out, g_xa, w_xq, g_xq, kx, vx, w_xo, seq, n_mem):
    n = x.shape[0]
    tm = ROW_TILE
    tiles_per_seq = seq // tm
    row = lambda width: pl.BlockSpec((tm, width), lambda i: (i, 0))
    memspec = pl.BlockSpec((n_mem, XA_WIDTH), lambda i: (i // tiles_per_seq, 0))
    return pl.pallas_call(
        _post_kernel,
        grid=(n // tm,),
        in_specs=[row(D_MODEL), row(CONV_CH), row(FOX_WIDTH), _const_spec(w_out.shape), _const_spec((1, D_MODEL)),
                  _const_spec(w_xq.shape), _const_spec((1, XA_HEAD_DIM)), memspec, memspec, _const_spec(w_xo.shape)],
        out_specs=row(D_MODEL),
        out_shape=jax.ShapeDtypeStruct((n, D_MODEL), F32),
        compiler_params=pltpu.CompilerParams(dimension_semantics=("arbitrary",), vmem_limit_bytes=VMEM_LIMIT),
        name="post",
    )(x, yc, yf, w_out, g_xa, w_xq, g_xq, kx, vx, w_xo)


def _swiglu_acc(h, wg_ref, wu_ref, wd_ref):
    acc = None
    for c in range(D_FF // FF_CHUNK):
        sl = slice(c * FF_CHUNK, (c + 1) * FF_CHUNK)
        gate = jnp.dot(h, wg_ref[:, sl], preferred_element_type=F32)
        up = jnp.dot(h, wu_ref[:, sl], preferred_element_type=F32)
        a = (gate * (1.0 / (1.0 + jnp.exp(-gate))) * up).astype(BF16)
        part = jnp.dot(a, wd_ref[sl, :], preferred_element_type=F32)
        acc = part if acc is None else acc + part
    return acc


def _ffn_kernel(x_ref, g_ref, wg_ref, wu_ref, wd_ref, o_ref):
    x = x_ref[...]
    h = _rms(x, g_ref[...]).astype(BF16)
    o_ref[...] = x + _swiglu_acc(h, wg_ref, wu_ref, wd_ref)


def _ffn(x, g, wg, wu, wd):
    n = x.shape[0]
    tm = ROW_TILE
    row = pl.BlockSpec((tm, D_MODEL), lambda i: (i, 0))
    return pl.pallas_call(
        _ffn_kernel,
        grid=(n // tm,),
        in_specs=[row, _const_spec((1, D_MODEL)), _const_spec(wg.shape), _const_spec(wu.shape),
                  _const_spec(wd.shape)],
        out_specs=row,
        out_shape=jax.ShapeDtypeStruct((n, D_MODEL), F32),
        compiler_params=pltpu.CompilerParams(dimension_semantics=("arbitrary",), vmem_limit_bytes=VMEM_LIMIT),
        name="ffn",
    )(x, g, wg, wu, wd)


def _route_kernel(x_ref, g_ref, wr_hi_ref, wr_lo_ref, br_ref, tri_ref, h_ref, meta_ref, cnt_ref):
    lane = lax.broadcasted_iota(jnp.int32, meta_ref.shape, 1).astype(F32)
    hf = _rms(x_ref[...], g_ref[...])
    h_hi, h_lo = _split2(hf)
    h_ref[...] = h_hi
    logits = (jnp.dot(h_hi, wr_hi_ref[...], preferred_element_type=F32)
              + jnp.dot(h_hi, wr_lo_ref[...], preferred_element_type=F32)
              + jnp.dot(h_lo, wr_hi_ref[...], preferred_element_type=F32)) + br_ref[...]
    logits = jnp.where(lane < N_EXPERTS, logits, -jnp.inf)
    m1 = jnp.max(logits, axis=1, keepdims=True)
    i1 = jnp.min(jnp.where(logits == m1, lane, LANES), axis=1, keepdims=True)
    rest = jnp.where(lane == i1, -jnp.inf, logits)
    m2 = jnp.max(rest, axis=1, keepdims=True)
    i2 = jnp.min(jnp.where(rest == m2, lane, LANES), axis=1, keepdims=True)
    e2 = jnp.exp(m2 - m1)
    w1 = 1.0 / (1.0 + e2)
    w2 = e2 * w1
    onehot = jnp.where((lane == i1) | (lane == i2), 1.0, 0.0)
    ranks = jnp.dot(tri_ref[...], onehot.astype(BF16), preferred_element_type=F32) - onehot
    lr1 = jnp.sum(jnp.where(lane == i1, ranks, 0.0), axis=1, keepdims=True)
    lr2 = jnp.sum(jnp.where(lane == i2, ranks, 0.0), axis=1, keepdims=True)

    def split3(w, first_lane):
        a1 = w.astype(BF16).astype(F32)
        a2 = (w - a1).astype(BF16).astype(F32)
        a3 = (w - a1 - a2).astype(BF16).astype(F32)
        return jnp.where(lane == first_lane, a1, jnp.where(lane == first_lane + 1, a2,
                         jnp.where(lane == first_lane + 2, a3, 0.0)))

    meta = jnp.where(lane == META_EXPERT, i1, jnp.where(lane == META_EXPERT + 1, i2, 0.0))
    meta = meta + jnp.where(lane == META_RANK, lr1, jnp.where(lane == META_RANK + 1, lr2, 0.0))
    meta_ref[...] = meta + split3(w1, META_GATE[0]) + split3(w2, META_GATE[1])
    cnt_ref[...] = jnp.sum(onehot, axis=0, keepdims=True)


def _route(x, g, wr_hi, wr_lo, b_r, tri):
    n = x.shape[0]
    tm = ROW_TILE
    row = lambda width: pl.BlockSpec((tm, width), lambda i: (i, 0))
    return pl.pallas_call(
        _route_kernel,
        grid=(n // tm,),
        in_specs=[row(D_MODEL), _const_spec((1, D_MODEL)), _const_spec(wr_hi.shape), _const_spec(wr_lo.shape),
                  _const_spec((1, LANES)), _const_spec(tri.shape)],
        out_specs=[row(D_MODEL), row(LANES), pl.BlockSpec((None, 1, LANES), lambda i: (i, 0, 0))],
        out_shape=[jax.ShapeDtypeStruct((n, D_MODEL), BF16), jax.ShapeDtypeStruct((n, LANES), F32),
                   jax.ShapeDtypeStruct((n // tm, 1, LANES), F32)],
        compiler_params=pltpu.CompilerParams(dimension_semantics=("arbitrary",), vmem_limit_bytes=VMEM_LIMIT),
        name="route",
    )(x, g, wr_hi, wr_lo, b_r, tri)


def _routing_tables(counts, n_exp_tiles):
    cnt = counts[:, 0, :N_EXPERTS].astype(jnp.int32)
    seg = (cnt + SEG_ALIGN - 1) // SEG_ALIGN * SEG_ALIGN
    loff = jnp.cumsum(seg, axis=1) - seg
    tot = jnp.sum(seg, axis=0)
    gsize = (tot + EXP_TILE - 1) // EXP_TILE * EXP_TILE
    gend = jnp.cumsum(gsize)
    gstart = gend - gsize
    sstart = gstart[None, :] + jnp.cumsum(seg, axis=0) - seg
    n_used = gend[-1] // EXP_TILE
    tile = jnp.arange(n_exp_tiles, dtype=jnp.int32)
    expert = jnp.sum((tile[:, None] * EXP_TILE >= gend[None, :]).astype(jnp.int32), axis=1)
    expert = jnp.where(tile < n_used, expert, expert[n_used - 1])
    flat = lambda a: a.reshape(-1).astype(jnp.int32)
    return dict(seg=flat(seg), loff=flat(loff), sstart=flat(sstart),
                pad_start=flat(jnp.concatenate([gstart + tot, gend[-1:]])),
                pad_len=flat(gsize - tot), expert=flat(expert), n_used=flat(n_used))


def _chunks(n_rows, bits):
    n = n_rows // SEG_ALIGN
    for b in range(bits):
        yield ((n >> b) & 1) == 1, (n & ((1 << b) - 1)) * SEG_ALIGN, SEG_ALIGN << b


def _segment_copies(i, seg_s, loff_s, sstart_s, make):
    for e in range(N_EXPERTS):
        k = i * N_EXPERTS + e
        for cond, off, size in _chunks(seg_s[k], SEG_BITS):
            yield cond, make(pl.multiple_of(loff_s[k] + off, SEG_ALIGN),
                             pl.multiple_of(sstart_s[k] + off, SEG_ALIGN), size)


def _start_copies(plan):
    for cond, copies in plan:
        @pl.when(cond)
        def _():
            for c in copies:
                c.start()


def _wait_copies(plan):
    for cond, copies in plan:
        @pl.when(cond)
        def _():
            for c in copies:
                c.wait()


def _local_slot(expert, rank, loff_s, i):
    off = jnp.zeros_like(rank)
    for e in range(N_EXPERTS):
        off = jnp.where(expert == e, loff_s[i * N_EXPERTS + e].astype(F32), off)
    return (off + rank).astype(jnp.int32)


def _dispatch_kernel(seg_s, loff_s, sstart_s, pstart_s, plen_s, h_ref, meta_ref, xs_hbm, gs_hbm,
                     xloc, gloc, zx, zg, sem):
    i = pl.program_id(0)
    tm = h_ref.shape[0]
    meta = meta_ref[...]
    meta_t = meta.T
    slot = lax.broadcasted_iota(jnp.int32, (LOCAL_SLOTS, tm), 0)
    p1, p2 = (slot == _local_slot(meta_t[META_EXPERT + c:META_EXPERT + c + 1, :],
                                  meta_t[META_RANK + c:META_RANK + c + 1, :], loff_s, i) for c in range(2))
    xloc[...] = jnp.dot(jnp.where(p1 | p2, 1.0, 0.0).astype(BF16), h_ref[...], preferred_element_type=F32)
    lane = lax.broadcasted_iota(jnp.int32, meta.shape, 1)
    gates = [jnp.where((lane >= first) & (lane < first + 3), meta, 0.0).astype(BF16) for first in META_GATE]
    gloc[...] = (jnp.dot(jnp.where(p1, 1.0, 0.0).astype(BF16), gates[0], preferred_element_type=F32)
                 + jnp.dot(jnp.where(p2, 1.0, 0.0).astype(BF16), gates[1], preferred_element_type=F32))

    def make(local, dst, size):
        return [pltpu.make_async_copy(xloc.at[pl.ds(local, size)], xs_hbm.at[pl.ds(dst, size)], sem.at[0]),
                pltpu.make_async_copy(gloc.at[pl.ds(local, size)], gs_hbm.at[pl.ds(dst, size)], sem.at[1])]

    plan = list(_segment_copies(i, seg_s, loff_s, sstart_s, make))
    _start_copies(plan)
    _wait_copies(plan)

    @pl.when(i == pl.num_programs(0) - 1)
    def _():
        zx[...] = jnp.zeros_like(zx)
        zg[...] = jnp.zeros_like(zg)
        plan = []
        for e in range(N_EXPERTS):
            for cond, off, size in _chunks(plen_s[e], PAD_BITS):
                dst = pl.multiple_of(pstart_s[e] + off, SEG_ALIGN)
                plan.append((cond, [
                    pltpu.make_async_copy(zx.at[pl.ds(0, size)], xs_hbm.at[pl.ds(dst, size)], sem.at[0]),
                    pltpu.make_async_copy(zg.at[pl.ds(0, size)], gs_hbm.at[pl.ds(dst, size)], sem.at[1])]))
        _start_copies(plan)
        _wait_copies(plan)

        pad_rows = zx.shape[0]
        tail = pstart_s[N_EXPERTS]

        def zero_chunk(c, _):
            dst = pl.multiple_of(tail + c * pad_rows, pad_rows)
            copies = [pltpu.make_async_copy(zx, xs_hbm.at[pl.ds(dst, pad_rows)], sem.at[0]),
                      pltpu.make_async_copy(zg, gs_hbm.at[pl.ds(dst, pad_rows)], sem.at[1])]
            for copy in copies:
                copy.start()
            for copy in copies:
                copy.wait()
            return 0

        lax.fori_loop(0, (xs_hbm.shape[0] - tail) // pad_rows, zero_chunk, 0)


def _dispatch(h, meta, tables, n_slots):
    n = h.shape[0]
    tm = ROW_TILE
    row = lambda width: pl.BlockSpec((tm, width), lambda i, *_: (i, 0))
    pad_rows = SEG_ALIGN << (PAD_BITS - 1)
    return pl.pallas_call(
        _dispatch_kernel,
        grid_spec=pltpu.PrefetchScalarGridSpec(
            num_scalar_prefetch=5, grid=(n // tm,),
            in_specs=[row(D_MODEL), row(LANES)],
            out_specs=[pl.BlockSpec(memory_space=pl.ANY)] * 2,
            scratch_shapes=[pltpu.VMEM((LOCAL_SLOTS, D_MODEL), F32), pltpu.VMEM((LOCAL_SLOTS, LANES), F32),
                            pltpu.VMEM((pad_rows, D_MODEL), F32), pltpu.VMEM((pad_rows, LANES), F32),
                            pltpu.SemaphoreType.DMA((2,))]),
        out_shape=[jax.ShapeDtypeStruct((n_slots, D_MODEL), F32), jax.ShapeDtypeStruct((n_slots, LANES), F32)],
        compiler_params=pltpu.CompilerParams(dimension_semantics=("arbitrary",), vmem_limit_bytes=VMEM_LIMIT),
        name="dispatch",
    )(tables["seg"], tables["loff"], tables["sstart"], tables["pad_start"], tables["pad_len"], h, meta)


def _expert_kernel(expert_s, n_used_s, x_ref, g_ref, wg_ref, wu_ref, wd_ref, y_ref):
    used = pl.program_id(0) < n_used_s[0]

    @pl.when(used)
    def _():
        gate = jnp.sum(g_ref[...], axis=1, keepdims=True)
        y_ref[...] = gate * _swiglu_acc(x_ref[...].astype(BF16), wg_ref, wu_ref, wd_ref)

    @pl.when(jnp.logical_not(used))
    def _():
        y_ref[...] = jnp.zeros_like(y_ref)


def _experts(xs, gs, tables, wg, wu, wd):
    n_slots = xs.shape[0]
    used = lambda i, expert, n_used: (jnp.minimum(i, n_used[0] - 1), 0)
    weight = lambda shape: pl.BlockSpec((None,) + shape, lambda i, expert, n_used: (expert[i], 0, 0))
    return pl.pallas_call(
        _expert_kernel,
        grid_spec=pltpu.PrefetchScalarGridSpec(
            num_scalar_prefetch=2, grid=(n_slots // EXP_TILE,),
            in_specs=[pl.BlockSpec((EXP_TILE, D_MODEL), used), pl.BlockSpec((EXP_TILE, LANES), used),
                      weight((D_MODEL, D_FF)), weight((D_MODEL, D_FF)), weight((D_FF, D_MODEL))],
            out_specs=pl.BlockSpec((EXP_TILE, D_MODEL), lambda i, expert, n_used: (i, 0))),
        out_shape=jax.ShapeDtypeStruct((n_slots, D_MODEL), F32),
        compiler_params=pltpu.CompilerParams(dimension_semantics=("arbitrary",), vmem_limit_bytes=VMEM_LIMIT),
        name="experts",
    )(tables["expert"], tables["n_used"], xs, gs, wg, wu, wd)


def _combine_kernel(seg_s, loff_s, sstart_s, x_ref, meta_ref, y_hbm, o_ref, yloc, sem):
    i = pl.program_id(0)
    tm = x_ref.shape[0]

    @pl.when(i == 0)
    def _():
        yloc[...] = jnp.zeros_like(yloc)

    def make(local, src, size):
        return [pltpu.make_async_copy(y_hbm.at[pl.ds(src, size)], yloc.at[pl.ds(local, size)], sem.at[0])]

    plan = list(_segment_copies(i, seg_s, loff_s, sstart_s, make))
    _start_copies(plan)
    meta = meta_ref[...]
    slot = lax.broadcasted_iota(jnp.int32, (tm, LOCAL_SLOTS), 1)
    p1, p2 = (slot == _local_slot(meta[:, META_EXPERT + c:META_EXPERT + c + 1],
                                  meta[:, META_RANK + c:META_RANK + c + 1], loff_s, i) for c in range(2))
    pick = jnp.where(p1 | p2, 1.0, 0.0).astype(BF16)
    _wait_copies(plan)
    hi, lo = _split2(yloc[...])
    o_ref[...] = (x_ref[...] + jnp.dot(pick, hi, preferred_element_type=F32)
                  + jnp.dot(pick, lo, preferred_element_type=F32))


def _combine(x, meta, ys, tables):
    n = x.shape[0]
    tm = ROW_TILE
    row = lambda width: pl.BlockSpec((tm, width), lambda i, *_: (i, 0))
    return pl.pallas_call(
        _combine_kernel,
        grid_spec=pltpu.PrefetchScalarGridSpec(
            num_scalar_prefetch=3, grid=(n // tm,),
            in_specs=[row(D_MODEL), row(LANES), pl.BlockSpec(memory_space=pl.ANY)],
            out_specs=row(D_MODEL),
            scratch_shapes=[pltpu.VMEM((LOCAL_SLOTS, D_MODEL), F32), pltpu.SemaphoreType.DMA((1,))]),
        out_shape=jax.ShapeDtypeStruct((n, D_MODEL), F32),
        compiler_params=pltpu.CompilerParams(dimension_semantics=("arbitrary",), vmem_limit_bytes=VMEM_LIMIT),
        name="combine",
    )(tables["seg"], tables["loff"], tables["sstart"], x, meta, ys)


def _moe(x, g, wr_hi, wr_lo, b_r, tri, wg, wu, wd):
    n = x.shape[0]
    n_tiles = n // ROW_TILE
    n_slots = 2 * n + n_tiles * N_EXPERTS * (SEG_ALIGN - 1) + N_EXPERTS * EXP_TILE
    n_slots = -(-n_slots // EXP_TILE) * EXP_TILE
    h, meta, counts = _route(x, g, wr_hi, wr_lo, b_r, tri)
    tables = _routing_tables(counts, n_slots // EXP_TILE)
    xs, gs = _dispatch(h, meta, tables, n_slots)
    ys = _experts(xs, gs, tables, wg, wu, wd)
    return _combine(x, meta, ys, tables)


def _pad_lanes(a, width=LANES):
    return jnp.pad(a, ((0, 0), (0, width - a.shape[1])))


def kernel(x, mem, g_mix, w_in, conv_w, b_f, g_q, g_k, g_conv_out, g_fox_out, w_out, g_xa, g_mem, w_xq, w_xkv,
           g_xq, g_xk, w_xo, g_ffn, w_gate, w_up, w_down, w_router, b_router, we_gate, we_up, we_down):
    batch, seq, _ = x.shape
    n_mem = mem.shape[1]
    xs = x.reshape(batch * seq, D_MODEL)
    mems = mem.reshape(batch * n_mem, D_MODEL)
    n_main = 3 * CONV_CH + 3 * FOX_WIDTH

    group = jnp.arange(MXU_DIM) // HEAD_DIM
    gmat = (group[:, None] == group[None, :]).astype(BF16)
    tri = jnp.tril(jnp.ones((ROW_TILE, ROW_TILE), BF16))
    place = _aug_placement()
    row1 = lambda a: a.reshape(1, -1)

    for l in range(DEPTH):
        w_f = _pad_lanes(jnp.tile(w_in[l][:, n_main:], (1, 3))).astype(BF16)
        b_f3 = _pad_lanes(jnp.tile(row1(b_f[l]), (1, 3)))
        yc, q, k0, k1, v0, v1 = _mix_in(
            xs, row1(g_mix[l]), w_in[l][:, :n_main].astype(BF16), w_f, conv_w[l], b_f3,
            jnp.tile(row1(g_q[l]), (1, FOX_HEADS)), jnp.tile(row1(g_k[l]), (1, FOX_HEADS)),
            row1(g_conv_out[l]), gmat, tri, place, seq)
        yf = _fox_attention(q, k0, k1, v0, v1, row1(g_fox_out[l]), batch, seq)
        kx, vx = _mem_kv(mems, row1(g_mem[l]), w_xkv[l].astype(BF16), row1(g_xk[l]))
        xs = _post(xs, yc, yf, w_out[l].astype(BF16), row1(g_xa[l]), w_xq[l].astype(BF16), row1(g_xq[l]),
                   kx, vx, w_xo[l].astype(BF16), seq, n_mem)
        i = l // 2
        if l % 2 == 0:
            xs = _ffn(xs, row1(g_ffn[l]), w_gate[i].astype(BF16), w_up[i].astype(BF16), w_down[i].astype(BF16))
        else:
            wr = _pad_lanes(w_router[i])
            wr_hi = wr.astype(BF16)
            wr_lo = (wr - wr_hi.astype(F32)).astype(BF16)
            xs = _moe(xs, row1(g_ffn[l]), wr_hi, wr_lo, _pad_lanes(row1(b_router[i])), tri,
                      we_gate[i].astype(BF16), we_up[i].astype(BF16), we_down[i].astype(BF16))
    return xs.reshape(batch, seq, D_MODEL)
```

```python
import functools

import jax
import jax.numpy as jnp
import numpy as np
from jax import lax
from jax.experimental import pallas as pl
from jax.experimental.pallas import tpu as pltpu

D_MODEL = 1024
DEPTH = 4
HEAD_DIM = 64
CONV_CH = 512
CONV_K = 3
FOX_HEADS = 8
FOX_WIDTH = 512
XA_HEADS = 4
XA_HEAD_DIM = 128
XA_WIDTH = 512
D_FF = 2816
N_EXPERTS = 8
EPS = 1e-6

LANES = 128
SUBLANES = 8
MXU_DIM = 256
VMEM_LIMIT = 56 * 1024 * 1024

ROW_TILE = 512
ATT_TILE = 512
FF_CHUNK = 256
EXP_TILE = 512
SEG_ALIGN = SUBLANES
SEG_BITS = (ROW_TILE // SEG_ALIGN).bit_length()
PAD_BITS = (EXP_TILE // SEG_ALIGN - 1).bit_length()
LOCAL_SLOTS = -(-(2 * ROW_TILE + N_EXPERTS * (SEG_ALIGN - 1)) // LANES) * LANES
META_EXPERT, META_RANK, META_GATE = 0, 4, (8, 16)

F32 = jnp.float32
BF16 = jnp.bfloat16


def _const_spec(shape):
    return pl.BlockSpec(shape, lambda *_: (0,) * len(shape), pipeline_mode=pl.Buffered(1))


def _rms(x, g):
    return x * lax.rsqrt(jnp.mean(x * x, axis=-1, keepdims=True) + EPS) * g


def _split2(x):
    hi = x.astype(BF16)
    lo = (x - hi.astype(F32)).astype(BF16)
    return hi, lo


def _group_rms(y, g, gmat):
    hi, lo = _split2(y * y)
    parts = []
    for c in range(y.shape[1] // MXU_DIM):
        sl = slice(c * MXU_DIM, (c + 1) * MXU_DIM)
        parts.append(jnp.dot(hi[:, sl], gmat, preferred_element_type=F32)
                     + jnp.dot(lo[:, sl], gmat, preferred_element_type=F32))
    ms = jnp.concatenate(parts, axis=1) * (1.0 / HEAD_DIM)
    return y * lax.rsqrt(ms + EPS) * g


def _mix_in_kernel(tiles_per_seq, x_ref, g_ref, w_ref, wf_ref, cw_ref, bf_ref, gq_ref, gk_ref, gc_ref,
                   gmat_ref, tri_ref, place_ref, yc_ref, q_ref, k0_ref, k1_ref, v0_ref, v1_ref, ubuf, ccarry):
    i = pl.program_id(0)
    tm = x_ref.shape[0]
    h = _rms(x_ref[...], g_ref[...]).astype(BF16)

    def proj(j):
        return jnp.dot(h, w_ref[:, j * CONV_CH:(j + 1) * CONV_CH], preferred_element_type=F32)

    @pl.when(i % tiles_per_seq == 0)
    def _():
        ubuf[0:SUBLANES, :] = jnp.zeros((SUBLANES, CONV_CH), F32)
        ccarry[...] = jnp.zeros_like(ccarry)

    ubuf[SUBLANES:SUBLANES + tm, :] = proj(1) * proj(2)
    y = (cw_ref[0:1, :] * ubuf[SUBLANES - 2:SUBLANES - 2 + tm, :]
         + cw_ref[1:2, :] * ubuf[SUBLANES - 1:SUBLANES - 1 + tm, :]
         + cw_ref[2:3, :] * ubuf[SUBLANES:SUBLANES + tm, :])
    ubuf[0:SUBLANES, :] = ubuf[tm:tm + SUBLANES, :]
    gmat = gmat_ref[...]
    yc_ref[...] = _group_rms(proj(0) * y, gc_ref[...], gmat).astype(BF16)

    q_ref[...] = (_group_rms(proj(3), gq_ref[...], gmat) * (HEAD_DIM ** -0.5)).astype(BF16)

    z = jnp.dot(h, wf_ref[...], preferred_element_type=F32) + bf_ref[...]
    logf = jnp.minimum(z, 0.0) - jnp.log1p(jnp.exp(-jnp.abs(z)))
    p1 = logf.astype(BF16)
    r1 = logf - p1.astype(F32)
    p2 = r1.astype(BF16)
    p3 = (r1 - p2.astype(F32)).astype(BF16)
    lane = lax.broadcasted_iota(jnp.int32, logf.shape, 1)
    parts = jnp.where(lane < FOX_HEADS, p1, jnp.where(lane < 2 * FOX_HEADS, p2, p3))
    cs = jnp.dot(tri_ref[...], parts, preferred_element_type=F32)
    cs = cs + pltpu.roll(cs, LANES - FOX_HEADS, axis=1) + pltpu.roll(cs, LANES - 2 * FOX_HEADS, axis=1)
    c = jnp.where(lane < FOX_HEADS, cs + ccarry[...], 0.0)
    ccarry[...] = c[tm - 1:tm, :]

    c3 = -(c + pltpu.roll(c, FOX_HEADS, axis=1) + pltpu.roll(c, 2 * FOX_HEADS, axis=1))
    n1 = c3.astype(BF16)
    r1 = c3 - n1.astype(F32)
    n2 = r1.astype(BF16)
    n3 = (r1 - n2.astype(F32)).astype(BF16)
    nparts = jnp.where(lane < FOX_HEADS, n1, jnp.where(lane < 2 * FOX_HEADS, n2, n3))
    kn = _group_rms(proj(4), gk_ref[...], gmat)
    vv = proj(5)
    lane_w = lax.broadcasted_iota(jnp.int32, (1, FOX_WIDTH), 1)
    for j, (k_ref, v_ref) in enumerate(((k0_ref, v0_ref), (k1_ref, v1_ref))):
        own = (lane_w & HEAD_DIM) == j * HEAD_DIM
        aug = jnp.dot(nparts, place_ref[j], preferred_element_type=F32)
        k_ref[...] = jnp.where(own, kn, aug).astype(BF16)
        ones_lane = (lane_w & (LANES - 1)) == (1 - j) * HEAD_DIM
        v_ref[...] = jnp.where(own, vv, jnp.where(ones_lane, 1.0, 0.0)).astype(BF16)


def _aug_placement():
    place = np.zeros((2, LANES, FOX_WIDTH), np.float32)
    for head in range(FOX_HEADS):
        pair, j = divmod(head, 2)
        for m in range(3):
            place[j, FOX_HEADS * m + head, pair * LANES + (1 - j) * HEAD_DIM + m] = 1.0
    return jnp.asarray(place, BF16)


def _mix_in(x, g, w_main, w_f, conv_w, b_f, gq, gk, gc, gmat, tri, place, seq):
    n = x.shape[0]
    tm = ROW_TILE
    row = lambda width: pl.BlockSpec((tm, width), lambda i: (i, 0))
    return pl.pallas_call(
        functools.partial(_mix_in_kernel, seq // tm),
        grid=(n // tm,),
        in_specs=[row(D_MODEL), _const_spec((1, D_MODEL)), _const_spec(w_main.shape), _const_spec(w_f.shape),
                  _const_spec(conv_w.shape), _const_spec((1, LANES)), _const_spec((1, CONV_CH)),
                  _const_spec((1, CONV_CH)), _const_spec((1, CONV_CH)), _const_spec(gmat.shape),
                  _const_spec(tri.shape), _const_spec(place.shape)],
        out_specs=[row(CONV_CH)] * 6,
        out_shape=[jax.ShapeDtypeStruct((n, CONV_CH), BF16)] * 6,
        scratch_shapes=[pltpu.VMEM((tm + SUBLANES, CONV_CH), F32), pltpu.VMEM((1, LANES), F32)],
        compiler_params=pltpu.CompilerParams(dimension_semantics=("arbitrary",), vmem_limit_bytes=VMEM_LIMIT),
        name="mix_in",
    )(x, g, w_main, w_f, conv_w, b_f, gq, gk, gc, gmat, tri, place)


def _fox_kernel(q_ref, k0_ref, k1_ref, v0_ref, v1_ref, g_ref, o_ref, sa_scr, sb_scr):
    tq = tk = ATT_TILE
    nq = q_ref.shape[0] // tq
    k_refs = (k0_ref, k1_ref)
    v_refs = (v0_ref, v1_ref)
    bufs = (sa_scr, sb_scr)
    lane = lax.broadcasted_iota(jnp.int32, (1, LANES), 1)
    own = [(lane & HEAD_DIM) == j * HEAD_DIM for j in range(2)]
    spare = [(1 - j) * HEAD_DIM for j in range(2)]
    ones = [jnp.where((lane >= spare[j]) & (lane < spare[j] + 3), 1.0, 0.0).astype(BF16) for j in range(2)]
    causal = lax.broadcasted_iota(jnp.int32, (tq, tk), 0) >= lax.broadcasted_iota(jnp.int32, (tq, tk), 1)
    gain = g_ref[...]

    def scores(qi, kv, s_scr):
        q = q_ref[qi * tq:(qi + 1) * tq, :]
        start = kv * tk if isinstance(kv, int) else pl.multiple_of(kv * tk, tk)
        for j in range(2):
            s_scr[j] = lax.dot_general(jnp.where(own[j], q, ones[j]), k_refs[j][pl.ds(start, tk), :],
                                       (((1,), (1,)), ((), ())), preferred_element_type=F32)

    def absorb(kv, s_scr, carry, on_diagonal=False):
        start = kv * tk if isinstance(kv, int) else pl.multiple_of(kv * tk, tk)
        new = []
        for j in range(2):
            m, acc = carry[j]
            s = s_scr[j]
            if on_diagonal:
                s = jnp.where(causal, s, -jnp.inf)
            m_new = jnp.maximum(m, jnp.max(s, axis=1, keepdims=True))
            p = jnp.exp(s - m_new).astype(BF16)
            acc = jnp.exp(m - m_new) * acc + jnp.dot(p, v_refs[j][pl.ds(start, tk), :], preferred_element_type=F32)
            new.append((m_new, acc))
        return tuple(new)

    def finish(qi, carry):
        out = None
        for j in range(2):
            acc = carry[j][1]
            l = jnp.sum(jnp.where(lane == spare[j], acc, 0.0), axis=1, keepdims=True)
            o = jnp.where(own[j], acc, 0.0)
            o = o * lax.rsqrt(jnp.sum(o * o, axis=1, keepdims=True) * (1.0 / HEAD_DIM) + EPS * (l * l))
            out = o if out is None else out + o
        o_ref[qi * tq:(qi + 1) * tq, :] = (out * gain).astype(BF16)

    init = tuple((jnp.full((tq, 1), -jnp.inf, F32), jnp.zeros((tq, LANES), F32)) for _ in range(2))
    t0 = 0
    scores(0, 0, bufs[0])
    for qi in range(nq):
        first, second = bufs[t0 % 2], bufs[(t0 + 1) % 2]

        def two_blocks(u, carry, qi=qi, first=first, second=second):
            kv = 2 * u
            scores(qi, kv + 1, second)
            carry = absorb(kv, first, carry)
            scores(qi, kv + 2, first)
            return absorb(kv + 1, second, carry)

        carry = lax.fori_loop(0, qi // 2, two_blocks, init)
        if qi % 2 == 1:
            scores(qi, qi, second)
            carry = absorb(qi - 1, first, carry)
            diagonal, free = second, first
        else:
            diagonal, free = first, second
        if qi + 1 < nq:
            scores(qi + 1, 0, free)
        finish(qi, absorb(qi, diagonal, carry, on_diagonal=True))
        t0 += qi + 1


def _fox_attention(q, k0, k1, v0, v1, g_fox, batch, seq):
    n = q.shape[0]
    spec = pl.BlockSpec((seq, LANES), lambda b, p: (b, p))
    return pl.pallas_call(
        _fox_kernel,
        grid=(batch, FOX_HEADS // 2),
        in_specs=[spec] * 5 + [pl.BlockSpec((1, LANES), lambda b, p: (0, p))],
        out_specs=spec,
        out_shape=jax.ShapeDtypeStruct((n, FOX_WIDTH), BF16),
        scratch_shapes=[pltpu.VMEM((2, ATT_TILE, ATT_TILE), F32)] * 2,
        compiler_params=pltpu.CompilerParams(dimension_semantics=("arbitrary",) * 2, vmem_limit_bytes=VMEM_LIMIT),
        name="fox_attn",
    )(q, k0, k1, v0, v1, g_fox)


def _mem_kv_kernel(m_ref, g_ref, w_ref, gk_ref, k_ref, v_ref):
    h = _rms(m_ref[...], g_ref[...]).astype(BF16)
    kv = jnp.dot(h, w_ref[...], preferred_element_type=F32)
    gk = gk_ref[...]
    ks = [_rms(kv[:, a * XA_HEAD_DIM:(a + 1) * XA_HEAD_DIM], gk) for a in range(XA_HEADS)]
    k_ref[...] = jnp.concatenate(ks, axis=1).astype(BF16)
    v_ref[...] = kv[:, XA_WIDTH:].astype(BF16)


def _mem_kv(mem, g, w_xkv, g_xk):
    n = mem.shape[0]
    tm = ROW_TILE
    row = lambda width: pl.BlockSpec((tm, width), lambda i: (i, 0))
    return pl.pallas_call(
        _mem_kv_kernel,
        grid=(n // tm,),
        in_specs=[row(D_MODEL), _const_spec((1, D_MODEL)), _const_spec(w_xkv.shape), _const_spec((1, XA_HEAD_DIM))],
        out_specs=[row(XA_WIDTH)] * 2,
        out_shape=[jax.ShapeDtypeStruct((n, XA_WIDTH), BF16)] * 2,
        compiler_params=pltpu.CompilerParams(dimension_semantics=("arbitrary",), vmem_limit_bytes=VMEM_LIMIT),
        name="mem_kv",
    )(mem, g, w_xkv, g_xk)


def _post_kernel(x_ref, yc_ref, yf_ref, wo_ref, g_ref, wq_ref, gq_ref, k_ref, v_ref, wxo_ref, *rest):
    o_ref = rest[-4] if len(rest) > 1 else rest[0]
    x1 = (x_ref[...]
          + jnp.dot(yc_ref[...], wo_ref[0:CONV_CH, :], preferred_element_type=F32)
          + jnp.dot(yf_ref[...], wo_ref[CONV_CH:, :], preferred_element_type=F32))
    hx = _rms(x1, g_ref[...]).astype(BF16)
    qx = jnp.dot(hx, wq_ref[...], preferred_element_type=F32)
    gq = gq_ref[...]
    heads = []
    for a in range(XA_HEADS):
        sl = slice(a * XA_HEAD_DIM, (a + 1) * XA_HEAD_DIM)
        qa = _rms(qx[:, sl], gq).astype(BF16)
        s = lax.dot_general(qa, k_ref[:, sl], (((1,), (1,)), ((), ())),
                            preferred_element_type=F32) * (XA_HEAD_DIM ** -0.5)
        e = jnp.exp(s - jnp.max(s, axis=1, keepdims=True))
        p = (e / jnp.sum(e, axis=1, keepdims=True)).astype(BF16)
        heads.append(jnp.dot(p, v_ref[:, sl], preferred_element_type=F32).astype(BF16))
    att = jnp.concatenate(heads, axis=1)
    x2 = x1 + jnp.dot(att, wxo_ref[...], preferred_element_type=F32)
    o_ref[...] = x2
    if len(rest) > 1:
        _route_tile(x2, *rest[:-4], *rest[-3:])


def _post(x, yc, yf, w_out, g_xa, w_xq, g_xq, kx, vx, w_xo, seq, n_mem, router=()):
    n = x.shape[0]
    tm = ROW_TILE
    tiles_per_seq = seq // tm
    row = lambda width: pl.BlockSpec((tm, width), lambda i: (i, 0))
    memspec = pl.BlockSpec((n_mem, XA_WIDTH), lambda i: (i // tiles_per_seq, 0))
    out_specs, out_shape = [row(D_MODEL)], [jax.ShapeDtypeStruct((n, D_MODEL), F32)]
    if router:
        out_specs += [row(D_MODEL), row(LANES), pl.BlockSpec((None, 1, LANES), lambda i: (i, 0, 0))]
        out_shape += [jax.ShapeDtypeStruct((n, D_MODEL), BF16), jax.ShapeDtypeStruct((n, LANES), F32),
                      jax.ShapeDtypeStruct((n // tm, 1, LANES), F32)]
    return pl.pallas_call(
        _post_kernel,
        grid=(n // tm,),
        in_specs=[row(D_MODEL), row(CONV_CH), row(FOX_WIDTH), _const_spec(w_out.shape), _const_spec((1, D_MODEL)),
                  _const_spec(w_xq.shape), _const_spec((1, XA_HEAD_DIM)), memspec, memspec, _const_spec(w_xo.shape)]
                 + [_const_spec(a.shape) for a in router],
        out_specs=out_specs,
        out_shape=out_shape,
        compiler_params=pltpu.CompilerParams(dimension_semantics=("arbitrary",), vmem_limit_bytes=VMEM_LIMIT),
        name="post",
    )(x, yc, yf, w_out, g_xa, w_xq, g_xq, kx, vx, w_xo, *router)


def _swiglu_acc(h, wg_ref, wu_ref, wd_ref):
    acc = None
    for c in range(D_FF // FF_CHUNK):
        sl = slice(c * FF_CHUNK, (c + 1) * FF_CHUNK)
        gate = jnp.dot(h, wg_ref[:, sl], preferred_element_type=F32)
        up = jnp.dot(h, wu_ref[:, sl], preferred_element_type=F32)
        a = (gate * (1.0 / (1.0 + jnp.exp(-gate))) * up).astype(BF16)
        part = jnp.dot(a, wd_ref[sl, :], preferred_element_type=F32)
        acc = part if acc is None else acc + part
    return acc


def _ffn_kernel(x_ref, g_ref, wg_ref, wu_ref, wd_ref, o_ref):
    x = x_ref[...]
    h = _rms(x, g_ref[...]).astype(BF16)
    o_ref[...] = x + _swiglu_acc(h, wg_ref, wu_ref, wd_ref)


def _ffn(x, g, wg, wu, wd):
    n = x.shape[0]
    tm = ROW_TILE
    row = pl.BlockSpec((tm, D_MODEL), lambda i: (i, 0))
    return pl.pallas_call(
        _ffn_kernel,
        grid=(n // tm,),
        in_specs=[row, _const_spec((1, D_MODEL)), _const_spec(wg.shape), _const_spec(wu.shape),
                  _const_spec(wd.shape)],
        out_specs=row,
        out_shape=jax.ShapeDtypeStruct((n, D_MODEL), F32),
        compiler_params=pltpu.CompilerParams(dimension_semantics=("arbitrary",), vmem_limit_bytes=VMEM_LIMIT),
        name="ffn",
    )(x, g, wg, wu, wd)


def _route_tile(x, g_ref, wr_hi_ref, wr_lo_ref, br_ref, tri_ref, h_ref, meta_ref, cnt_ref):
    lane = lax.broadcasted_iota(jnp.int32, meta_ref.shape, 1).astype(F32)
    hf = _rms(x, g_ref[...])
    h_hi, h_lo = _split2(hf)
    h_ref[...] = h_hi
    logits = (jnp.dot(h_hi, wr_hi_ref[...], preferred_element_type=F32)
              + jnp.dot(h_hi, wr_lo_ref[...], preferred_element_type=F32)
              + jnp.dot(h_lo, wr_hi_ref[...], preferred_element_type=F32)) + br_ref[...]
    logits = jnp.where(lane < N_EXPERTS, logits, -jnp.inf)
    m1 = jnp.max(logits, axis=1, keepdims=True)
    i1 = jnp.min(jnp.where(logits == m1, lane, LANES), axis=1, keepdims=True)
    rest = jnp.where(lane == i1, -jnp.inf, logits)
    m2 = jnp.max(rest, axis=1, keepdims=True)
    i2 = jnp.min(jnp.where(rest == m2, lane, LANES), axis=1, keepdims=True)
    e2 = jnp.exp(m2 - m1)
    w1 = 1.0 / (1.0 + e2)
    w2 = e2 * w1
    onehot = jnp.where((lane == i1) | (lane == i2), 1.0, 0.0)
    ranks = jnp.dot(tri_ref[...], onehot.astype(BF16), preferred_element_type=F32) - onehot
    lr1 = jnp.sum(jnp.where(lane == i1, ranks, 0.0), axis=1, keepdims=True)
    lr2 = jnp.sum(jnp.where(lane == i2, ranks, 0.0), axis=1, keepdims=True)

    def split3(w, first_lane):
        a1 = w.astype(BF16).astype(F32)
        a2 = (w - a1).astype(BF16).astype(F32)
        a3 = (w - a1 - a2).astype(BF16).astype(F32)
        return jnp.where(lane == first_lane, a1, jnp.where(lane == first_lane + 1, a2,
                         jnp.where(lane == first_lane + 2, a3, 0.0)))

    meta = jnp.where(lane == META_EXPERT, i1, jnp.where(lane == META_EXPERT + 1, i2, 0.0))
    meta = meta + jnp.where(lane == META_RANK, lr1, jnp.where(lane == META_RANK + 1, lr2, 0.0))
    meta_ref[...] = meta + split3(w1, META_GATE[0]) + split3(w2, META_GATE[1])
    cnt_ref[...] = jnp.sum(onehot, axis=0, keepdims=True)


def _routing_tables(counts, n_exp_tiles):
    cnt = counts[:, 0, :N_EXPERTS].astype(jnp.int32)
    seg = (cnt + SEG_ALIGN - 1) // SEG_ALIGN * SEG_ALIGN
    loff = jnp.cumsum(seg, axis=1) - seg
    tot = jnp.sum(seg, axis=0)
    gsize = (tot + EXP_TILE - 1) // EXP_TILE * EXP_TILE
    gend = jnp.cumsum(gsize)
    gstart = gend - gsize
    sstart = gstart[None, :] + jnp.cumsum(seg, axis=0) - seg
    n_used = gend[-1] // EXP_TILE
    tile = jnp.arange(n_exp_tiles, dtype=jnp.int32)
    expert = jnp.sum((tile[:, None] * EXP_TILE >= gend[None, :]).astype(jnp.int32), axis=1)
    expert = jnp.where(tile < n_used, expert, expert[n_used - 1])
    flat = lambda a: a.reshape(-1).astype(jnp.int32)
    return dict(seg=flat(seg), loff=flat(loff), sstart=flat(sstart),
                pad_start=flat(jnp.concatenate([gstart + tot, gend[-1:]])),
                pad_len=flat(gsize - tot), expert=flat(expert), n_used=flat(n_used))


def _chunks(n_rows, bits):
    n = n_rows // SEG_ALIGN
    for b in range(bits):
        yield ((n >> b) & 1) == 1, (n & ((1 << b) - 1)) * SEG_ALIGN, SEG_ALIGN << b


def _segment_copies(i, seg_s, loff_s, sstart_s, make):
    for e in range(N_EXPERTS):
        k = i * N_EXPERTS + e
        for cond, off, size in _chunks(seg_s[k], SEG_BITS):
            yield cond, make(pl.multiple_of(loff_s[k] + off, SEG_ALIGN),
                             pl.multiple_of(sstart_s[k] + off, SEG_ALIGN), size)


def _start_copies(plan):
    for cond, copies in plan:
        @pl.when(cond)
        def _():
            for c in copies:
                c.start()


def _wait_copies(plan):
    for cond, copies in plan:
        @pl.when(cond)
        def _():
            for c in copies:
                c.wait()


def _local_slot(expert, rank, loff_s, i):
    off = jnp.zeros_like(rank)
    for e in range(N_EXPERTS):
        off = jnp.where(expert == e, loff_s[i * N_EXPERTS + e].astype(F32), off)
    return (off + rank).astype(jnp.int32)


def _dispatch_kernel(seg_s, loff_s, sstart_s, pstart_s, plen_s, h_ref, meta_ref, xs_hbm, gs_hbm,
                     xloc, gloc, zx, zg, sem):
    i = pl.program_id(0)
    last = pl.num_programs(0) - 1
    tm = h_ref.shape[0]
    buf = i % 2
    meta = meta_ref[...]
    meta_t = meta.T
    slot = lax.broadcasted_iota(jnp.int32, (LOCAL_SLOTS, tm), 0)
    p1, p2 = (slot == _local_slot(meta_t[META_EXPERT + c:META_EXPERT + c + 1, :],
                                  meta_t[META_RANK + c:META_RANK + c + 1, :], loff_s, i) for c in range(2))
    xloc[buf] = jnp.dot(jnp.where(p1 | p2, 1.0, 0.0).astype(BF16), h_ref[...], preferred_element_type=F32)
    lane = lax.broadcasted_iota(jnp.int32, meta.shape, 1)
    gates = [jnp.where((lane >= first) & (lane < first + 3), meta, 0.0).astype(BF16) for first in META_GATE]
    gloc[buf] = (jnp.dot(jnp.where(p1, 1.0, 0.0).astype(BF16), gates[0], preferred_element_type=F32)
                 + jnp.dot(jnp.where(p2, 1.0, 0.0).astype(BF16), gates[1], preferred_element_type=F32))

    def plan(tile, b):
        def make(local, dst, size):
            return [pltpu.make_async_copy(xloc.at[b, pl.ds(local, size)], xs_hbm.at[pl.ds(dst, size)], sem.at[b, 0]),
                    pltpu.make_async_copy(gloc.at[b, pl.ds(local, size)], gs_hbm.at[pl.ds(dst, size)], sem.at[b, 1])]
        return list(_segment_copies(tile, seg_s, loff_s, sstart_s, make))

    _start_copies(plan(i, buf))

    @pl.when(i > 0)
    def _():
        _wait_copies(plan(i - 1, 1 - buf))

    @pl.when(i == last)
    def _():
        _wait_copies(plan(i, buf))
        zx[...] = jnp.zeros_like(zx)
        zg[...] = jnp.zeros_like(zg)
        fill = []
        for e in range(N_EXPERTS):
            for cond, off, size in _chunks(plen_s[e], PAD_BITS):
                dst = pl.multiple_of(pstart_s[e] + off, SEG_ALIGN)
                fill.append((cond, [
                    pltpu.make_async_copy(zx.at[pl.ds(0, size)], xs_hbm.at[pl.ds(dst, size)], sem.at[0, 0]),
                    pltpu.make_async_copy(zg.at[pl.ds(0, size)], gs_hbm.at[pl.ds(dst, size)], sem.at[0, 1])]))
        _start_copies(fill)
        _wait_copies(fill)

        pad_rows = zx.shape[0]
        tail = pstart_s[N_EXPERTS]

        def zero_chunk(c, _):
            dst = pl.multiple_of(tail + c * pad_rows, pad_rows)
            copies = [pltpu.make_async_copy(zx, xs_hbm.at[pl.ds(dst, pad_rows)], sem.at[0, 0]),
                      pltpu.make_async_copy(zg, gs_hbm.at[pl.ds(dst, pad_rows)], sem.at[0, 1])]
            for copy in copies:
                copy.start()
            for copy in copies:
                copy.wait()
            return 0

        lax.fori_loop(0, (xs_hbm.shape[0] - tail) // pad_rows, zero_chunk, 0)


def _dispatch(h, meta, tables, n_slots):
    n = h.shape[0]
    tm = ROW_TILE
    row = lambda width: pl.BlockSpec((tm, width), lambda i, *_: (i, 0))
    pad_rows = SEG_ALIGN << (PAD_BITS - 1)
    return pl.pallas_call(
        _dispatch_kernel,
        grid_spec=pltpu.PrefetchScalarGridSpec(
            num_scalar_prefetch=5, grid=(n // tm,),
            in_specs=[row(D_MODEL), row(LANES)],
            out_specs=[pl.BlockSpec(memory_space=pl.ANY)] * 2,
            scratch_shapes=[pltpu.VMEM((2, LOCAL_SLOTS, D_MODEL), F32), pltpu.VMEM((2, LOCAL_SLOTS, LANES), F32),
                            pltpu.VMEM((pad_rows, D_MODEL), F32), pltpu.VMEM((pad_rows, LANES), F32),
                            pltpu.SemaphoreType.DMA((2, 2))]),
        out_shape=[jax.ShapeDtypeStruct((n_slots, D_MODEL), F32), jax.ShapeDtypeStruct((n_slots, LANES), F32)],
        compiler_params=pltpu.CompilerParams(dimension_semantics=("arbitrary",), vmem_limit_bytes=VMEM_LIMIT),
        name="dispatch",
    )(tables["seg"], tables["loff"], tables["sstart"], tables["pad_start"], tables["pad_len"], h, meta)


def _expert_kernel(expert_s, n_used_s, x_ref, g_ref, wg_ref, wu_ref, wd_ref, y_ref):
    used = pl.program_id(0) < n_used_s[0]

    @pl.when(used)
    def _():
        gate = jnp.sum(g_ref[...], axis=1, keepdims=True)
        y_ref[...] = gate * _swiglu_acc(x_ref[...].astype(BF16), wg_ref, wu_ref, wd_ref)

    @pl.when(jnp.logical_not(used))
    def _():
        y_ref[...] = jnp.zeros_like(y_ref)


def _experts(xs, gs, tables, wg, wu, wd):
    n_slots = xs.shape[0]
    used = lambda i, expert, n_used: (jnp.minimum(i, n_used[0] - 1), 0)
    weight = lambda shape: pl.BlockSpec((None,) + shape, lambda i, expert, n_used: (expert[i], 0, 0))
    return pl.pallas_call(
        _expert_kernel,
        grid_spec=pltpu.PrefetchScalarGridSpec(
            num_scalar_prefetch=2, grid=(n_slots // EXP_TILE,),
            in_specs=[pl.BlockSpec((EXP_TILE, D_MODEL), used), pl.BlockSpec((EXP_TILE, LANES), used),
                      weight((D_MODEL, D_FF)), weight((D_MODEL, D_FF)), weight((D_FF, D_MODEL))],
            out_specs=pl.BlockSpec((EXP_TILE, D_MODEL), lambda i, expert, n_used: (i, 0))),
        out_shape=jax.ShapeDtypeStruct((n_slots, D_MODEL), F32),
        compiler_params=pltpu.CompilerParams(dimension_semantics=("arbitrary",), vmem_limit_bytes=VMEM_LIMIT),
        name="experts",
    )(tables["expert"], tables["n_used"], xs, gs, wg, wu, wd)


def _combine_kernel(seg_s, loff_s, sstart_s, x_ref, meta_ref, y_hbm, o_ref, yloc, sem):
    i = pl.program_id(0)
    tm = x_ref.shape[0]
    buf = i % 2

    def plan(tile, b):
        def make(local, src, size):
            return [pltpu.make_async_copy(y_hbm.at[pl.ds(src, size)], yloc.at[b, pl.ds(local, size)], sem.at[b])]
        return list(_segment_copies(tile, seg_s, loff_s, sstart_s, make))

    @pl.when(i == 0)
    def _():
        yloc[...] = jnp.zeros_like(yloc)
        _start_copies(plan(0, 0))

    @pl.when(i + 1 < pl.num_programs(0))
    def _():
        _start_copies(plan(i + 1, 1 - buf))

    meta = meta_ref[...]
    slot = lax.broadcasted_iota(jnp.int32, (tm, LOCAL_SLOTS), 1)
    p1, p2 = (slot == _local_slot(meta[:, META_EXPERT + c:META_EXPERT + c + 1],
                                  meta[:, META_RANK + c:META_RANK + c + 1], loff_s, i) for c in range(2))
    pick = jnp.where(p1 | p2, 1.0, 0.0).astype(BF16)
    _wait_copies(plan(i, buf))
    hi, lo = _split2(yloc[buf])
    o_ref[...] = (x_ref[...] + jnp.dot(pick, hi, preferred_element_type=F32)
                  + jnp.dot(pick, lo, preferred_element_type=F32))


def _combine(x, meta, ys, tables):
    n = x.shape[0]
    tm = ROW_TILE
    row = lambda width: pl.BlockSpec((tm, width), lambda i, *_: (i, 0))
    return pl.pallas_call(
        _combine_kernel,
        grid_spec=pltpu.PrefetchScalarGridSpec(
            num_scalar_prefetch=3, grid=(n // tm,),
            in_specs=[row(D_MODEL), row(LANES), pl.BlockSpec(memory_space=pl.ANY)],
            out_specs=row(D_MODEL),
            scratch_shapes=[pltpu.VMEM((2, LOCAL_SLOTS, D_MODEL), F32), pltpu.SemaphoreType.DMA((2,))]),
        out_shape=jax.ShapeDtypeStruct((n, D_MODEL), F32),
        compiler_params=pltpu.CompilerParams(dimension_semantics=("arbitrary",), vmem_limit_bytes=VMEM_LIMIT),
        name="combine",
    )(tables["seg"], tables["loff"], tables["sstart"], x, meta, ys)


def _moe(x, h, meta, counts, wg, wu, wd):
    n = x.shape[0]
    n_tiles = n // ROW_TILE
    n_slots = 2 * n + n_tiles * N_EXPERTS * (SEG_ALIGN - 1) + N_EXPERTS * EXP_TILE
    n_slots = -(-n_slots // EXP_TILE) * EXP_TILE
    tables = _routing_tables(counts, n_slots // EXP_TILE)
    xs, gs = _dispatch(h, meta, tables, n_slots)
    ys = _experts(xs, gs, tables, wg, wu, wd)
    return _combine(x, meta, ys, tables)


def _pad_lanes(a, width=LANES):
    return jnp.pad(a, ((0, 0), (0, width - a.shape[1])))


def kernel(x, mem, g_mix, w_in, conv_w, b_f, g_q, g_k, g_conv_out, g_fox_out, w_out, g_xa, g_mem, w_xq, w_xkv,
           g_xq, g_xk, w_xo, g_ffn, w_gate, w_up, w_down, w_router, b_router, we_gate, we_up, we_down):
    batch, seq, _ = x.shape
    n_mem = mem.shape[1]
    xs = x.reshape(batch * seq, D_MODEL)
    mems = mem.reshape(batch * n_mem, D_MODEL)
    n_main = 3 * CONV_CH + 3 * FOX_WIDTH

    group = jnp.arange(MXU_DIM) // HEAD_DIM
    gmat = (group[:, None] == group[None, :]).astype(BF16)
    tri = jnp.tril(jnp.ones((ROW_TILE, ROW_TILE), BF16))
    place = _aug_placement()
    row1 = lambda a: a.reshape(1, -1)

    for l in range(DEPTH):
        w_f = _pad_lanes(jnp.tile(w_in[l][:, n_main:], (1, 3))).astype(BF16)
        b_f3 = _pad_lanes(jnp.tile(row1(b_f[l]), (1, 3)))
        yc, q, k0, k1, v0, v1 = _mix_in(
            xs, row1(g_mix[l]), w_in[l][:, :n_main].astype(BF16), w_f, conv_w[l], b_f3,
            jnp.tile(row1(g_q[l]), (1, FOX_HEADS)), jnp.tile(row1(g_k[l]), (1, FOX_HEADS)),
            row1(g_conv_out[l]), gmat, tri, place, seq)
        yf = _fox_attention(q, k0, k1, v0, v1, row1(g_fox_out[l]), batch, seq)
        kx, vx = _mem_kv(mems, row1(g_mem[l]), w_xkv[l].astype(BF16), row1(g_xk[l]))
        i = l // 2
        router = ()
        if l % 2 == 1:
            wr = _pad_lanes(w_router[i])
            wr_hi = wr.astype(BF16)
            wr_lo = (wr - wr_hi.astype(F32)).astype(BF16)
            router = (row1(g_ffn[l]), wr_hi, wr_lo, _pad_lanes(row1(b_router[i])), tri)
        res = _post(xs, yc, yf, w_out[l].astype(BF16), row1(g_xa[l]), w_xq[l].astype(BF16), row1(g_xq[l]),
                    kx, vx, w_xo[l].astype(BF16), seq, n_mem, router)
        if l % 2 == 0:
            xs = _ffn(res[0], row1(g_ffn[l]), w_gate[i].astype(BF16), w_up[i].astype(BF16), w_down[i].astype(BF16))
        else:
            xs = _moe(*res, we_gate[i].astype(BF16), we_up[i].astype(BF16), we_down[i].astype(BF16))
    return xs.reshape(batch, seq, D_MODEL)
```

```python
import functools

import jax
import jax.numpy as jnp
import numpy as np
from jax import lax
from jax.experimental import pallas as pl
from jax.experimental.pallas import tpu as pltpu

D_MODEL = 1024
DEPTH = 4
HEAD_DIM = 64
CONV_CH = 512
CONV_K = 3
FOX_HEADS = 8
FOX_WIDTH = 512
XA_HEADS = 4
XA_HEAD_DIM = 128
XA_WIDTH = 512
D_FF = 2816
N_EXPERTS = 8
EPS = 1e-6

LANES = 128
SUBLANES = 8
MXU_DIM = 256
VMEM_LIMIT = 56 * 1024 * 1024

ROW_TILE = 512
ATT_TILE = 512
FF_CHUNK = 256
EXP_TILE = 512
SEG_ALIGN = SUBLANES
SEG_BITS = (ROW_TILE // SEG_ALIGN).bit_length()
PAD_BITS = (EXP_TILE // SEG_ALIGN - 1).bit_length()
LOCAL_SLOTS = -(-(2 * ROW_TILE + N_EXPERTS * (SEG_ALIGN - 1)) // LANES) * LANES
META_EXPERT, META_RANK, META_GATE = 0, 4, (8, 16)

F32 = jnp.float32
BF16 = jnp.bfloat16


def _const_spec(shape):
    return pl.BlockSpec(shape, lambda *_: (0,) * len(shape), pipeline_mode=pl.Buffered(1))


def _rms(x, g):
    return x * lax.rsqrt(jnp.mean(x * x, axis=-1, keepdims=True) + EPS) * g


def _split2(x):
    hi = x.astype(BF16)
    lo = (x - hi.astype(F32)).astype(BF16)
    return hi, lo


def _group_rms(y, g, gmat):
    sq = (y * y).astype(BF16)
    parts = [jnp.dot(sq[:, c * MXU_DIM:(c + 1) * MXU_DIM], gmat, preferred_element_type=F32)
             for c in range(y.shape[1] // MXU_DIM)]
    ms = jnp.concatenate(parts, axis=1) * (1.0 / HEAD_DIM)
    return y * lax.rsqrt(ms + EPS) * g


def _mix_in_kernel(tiles_per_seq, x_ref, g_ref, w_ref, wf_ref, cw_ref, bf_ref, gq_ref, gk_ref, gc_ref,
                   gmat_ref, tri_ref, place_ref, yc_ref, q_ref, k0_ref, k1_ref, v0_ref, v1_ref, ubuf, ccarry):
    i = pl.program_id(0)
    tm = x_ref.shape[0]
    h = _rms(x_ref[...], g_ref[...]).astype(BF16)

    def proj(j):
        return jnp.dot(h, w_ref[:, j * CONV_CH:(j + 1) * CONV_CH], preferred_element_type=F32)

    @pl.when(i % tiles_per_seq == 0)
    def _():
        ubuf[0:SUBLANES, :] = jnp.zeros((SUBLANES, CONV_CH), F32)
        ccarry[...] = jnp.zeros_like(ccarry)

    ubuf[SUBLANES:SUBLANES + tm, :] = proj(1) * proj(2)
    y = (cw_ref[0:1, :] * ubuf[SUBLANES - 2:SUBLANES - 2 + tm, :]
         + cw_ref[1:2, :] * ubuf[SUBLANES - 1:SUBLANES - 1 + tm, :]
         + cw_ref[2:3, :] * ubuf[SUBLANES:SUBLANES + tm, :])
    ubuf[0:SUBLANES, :] = ubuf[tm:tm + SUBLANES, :]
    gmat = gmat_ref[...]
    yc_ref[...] = _group_rms(proj(0) * y, gc_ref[...], gmat).astype(BF16)

    q_ref[...] = (_group_rms(proj(3), gq_ref[...], gmat) * (HEAD_DIM ** -0.5)).astype(BF16)

    z = jnp.dot(h, wf_ref[...], preferred_element_type=F32) + bf_ref[...]
    logf = jnp.minimum(z, 0.0) - jnp.log1p(jnp.exp(-jnp.abs(z)))
    p1 = logf.astype(BF16)
    r1 = logf - p1.astype(F32)
    p2 = r1.astype(BF16)
    p3 = (r1 - p2.astype(F32)).astype(BF16)
    lane = lax.broadcasted_iota(jnp.int32, logf.shape, 1)
    parts = jnp.where(lane < FOX_HEADS, p1, jnp.where(lane < 2 * FOX_HEADS, p2, p3))
    cs = jnp.dot(tri_ref[...], parts, preferred_element_type=F32)
    cs = cs + pltpu.roll(cs, LANES - FOX_HEADS, axis=1) + pltpu.roll(cs, LANES - 2 * FOX_HEADS, axis=1)
    c = jnp.where(lane < FOX_HEADS, cs + ccarry[...], 0.0)
    ccarry[...] = c[tm - 1:tm, :]

    c3 = -(c + pltpu.roll(c, FOX_HEADS, axis=1) + pltpu.roll(c, 2 * FOX_HEADS, axis=1))
    n1 = c3.astype(BF16)
    r1 = c3 - n1.astype(F32)
    n2 = r1.astype(BF16)
    n3 = (r1 - n2.astype(F32)).astype(BF16)
    nparts = jnp.where(lane < FOX_HEADS, n1, jnp.where(lane < 2 * FOX_HEADS, n2, n3))
    kn = _group_rms(proj(4), gk_ref[...], gmat)
    vv = proj(5)
    lane_w = lax.broadcasted_iota(jnp.int32, (1, FOX_WIDTH), 1)
    for j, (k_ref, v_ref) in enumerate(((k0_ref, v0_ref), (k1_ref, v1_ref))):
        own = (lane_w & HEAD_DIM) == j * HEAD_DIM
        aug = jnp.dot(nparts, place_ref[j], preferred_element_type=F32)
        k_ref[...] = jnp.where(own, kn, aug).astype(BF16)
        ones_lane = (lane_w & (LANES - 1)) == (1 - j) * HEAD_DIM
        v_ref[...] = jnp.where(own, vv, jnp.where(ones_lane, 1.0, 0.0)).astype(BF16)


def _aug_placement():
    place = np.zeros((2, LANES, FOX_WIDTH), np.float32)
    for head in range(FOX_HEADS):
        pair, j = divmod(head, 2)
        for m in range(3):
            place[j, FOX_HEADS * m + head, pair * LANES + (1 - j) * HEAD_DIM + m] = 1.0
    return jnp.asarray(place, BF16)


def _mix_in(x, g, w_main, w_f, conv_w, b_f, gq, gk, gc, gmat, tri, place, seq):
    n = x.shape[0]
    tm = ROW_TILE
    row = lambda width: pl.BlockSpec((tm, width), lambda i: (i, 0))
    return pl.pallas_call(
        functools.partial(_mix_in_kernel, seq // tm),
        grid=(n // tm,),
        in_specs=[row(D_MODEL), _const_spec((1, D_MODEL)), _const_spec(w_main.shape), _const_spec(w_f.shape),
                  _const_spec(conv_w.shape), _const_spec((1, LANES)), _const_spec((1, CONV_CH)),
                  _const_spec((1, CONV_CH)), _const_spec((1, CONV_CH)), _const_spec(gmat.shape),
                  _const_spec(tri.shape), _const_spec(place.shape)],
        out_specs=[row(CONV_CH)] * 6,
        out_shape=[jax.ShapeDtypeStruct((n, CONV_CH), BF16)] * 6,
        scratch_shapes=[pltpu.VMEM((tm + SUBLANES, CONV_CH), F32), pltpu.VMEM((1, LANES), F32)],
        compiler_params=pltpu.CompilerParams(dimension_semantics=("arbitrary",), vmem_limit_bytes=VMEM_LIMIT),
        name="mix_in",
    )(x, g, w_main, w_f, conv_w, b_f, gq, gk, gc, gmat, tri, place)


def _fox_kernel(q_ref, k0_ref, k1_ref, v0_ref, v1_ref, g_ref, o_ref, sa_scr, sb_scr, m_scr, acc_scr):
    tq = tk = ATT_TILE
    nq = q_ref.shape[0] // tq
    k_refs = (k0_ref, k1_ref)
    v_refs = (v0_ref, v1_ref)
    bufs = (sa_scr, sb_scr)
    lane = lax.broadcasted_iota(jnp.int32, (1, LANES), 1)
    own = [(lane & HEAD_DIM) == j * HEAD_DIM for j in range(2)]
    spare = [(1 - j) * HEAD_DIM for j in range(2)]
    ones = [jnp.where((lane >= spare[j]) & (lane < spare[j] + 3), 1.0, 0.0).astype(BF16) for j in range(2)]
    causal = lax.broadcasted_iota(jnp.int32, (tq, tk), 0) >= lax.broadcasted_iota(jnp.int32, (tq, tk), 1)
    gain = g_ref[...]

    def scores(qi, kv, s_scr):
        q = q_ref[qi * tq:(qi + 1) * tq, :]
        start = kv * tk if isinstance(kv, int) else pl.multiple_of(kv * tk, tk)
        for j in range(2):
            s_scr[j] = lax.dot_general(jnp.where(own[j], q, ones[j]), k_refs[j][pl.ds(start, tk), :],
                                       (((1,), (1,)), ((), ())), preferred_element_type=F32)

    def absorb(kv, s_scr, carry, on_diagonal=False):
        start = kv * tk if isinstance(kv, int) else pl.multiple_of(kv * tk, tk)
        new = []
        for j in range(2):
            m, acc = carry[j]
            s = s_scr[j]
            if on_diagonal:
                s = jnp.where(causal, s, -jnp.inf)
            m_new = jnp.maximum(m, jnp.max(s, axis=1, keepdims=True))
            p = jnp.exp(s - m_new).astype(BF16)
            acc = jnp.exp(m - m_new) * acc + jnp.dot(p, v_refs[j][pl.ds(start, tk), :], preferred_element_type=F32)
            new.append((m_new, acc))
        return tuple(new)

    def load_state():
        return tuple((m_scr[j], acc_scr[j]) for j in range(2))

    def store_state(carry):
        for j in range(2):
            m_scr[j], acc_scr[j] = carry[j]

    def finish(qi, carry):
        out = None
        for j in range(2):
            acc = carry[j][1]
            l = jnp.sum(jnp.where(lane == spare[j], acc, 0.0), axis=1, keepdims=True)
            o = jnp.where(own[j], acc, 0.0)
            o = o * lax.rsqrt(jnp.sum(o * o, axis=1, keepdims=True) * (1.0 / HEAD_DIM) + EPS * (l * l))
            out = o if out is None else out + o
        o_ref[qi * tq:(qi + 1) * tq, :] = (out * gain).astype(BF16)

    init = tuple((jnp.full((tq, 1), -jnp.inf, F32), jnp.zeros((tq, LANES), F32)) for _ in range(2))
    t0 = 0
    scores(0, 0, bufs[0])
    for qi in range(nq):
        first, second = bufs[t0 % 2], bufs[(t0 + 1) % 2]

        def two_blocks(kv, carry, qi=qi, first=first, second=second):
            scores(qi, kv + 1, second)
            carry = absorb(kv, first, carry)
            scores(qi, kv + 2, first)
            return absorb(kv + 1, second, carry)

        def two_blocks_in_place(u, _, two_blocks=two_blocks):
            store_state(two_blocks(2 * u, load_state()))
            return 0

        carry = init
        if qi // 2 == 1:
            carry = two_blocks(0, carry)
        elif qi // 2 > 1:
            store_state(carry)
            lax.fori_loop(0, qi // 2, two_blocks_in_place, 0)
            carry = load_state()
        if qi % 2 == 1:
            scores(qi, qi, second)
            carry = absorb(qi - 1, first, carry)
            diagonal, free = second, first
        else:
            diagonal, free = first, second
        if qi + 1 < nq:
            scores(qi + 1, 0, free)
        finish(qi, absorb(qi, diagonal, carry, on_diagonal=True))
        t0 += qi + 1


def _fox_attention(q, k0, k1, v0, v1, g_fox, batch, seq):
    n = q.shape[0]
    spec = pl.BlockSpec((seq, LANES), lambda b, p: (b, p))
    return pl.pallas_call(
        _fox_kernel,
        grid=(batch, FOX_HEADS // 2),
        in_specs=[spec] * 5 + [pl.BlockSpec((1, LANES), lambda b, p: (0, p))],
        out_specs=spec,
        out_shape=jax.ShapeDtypeStruct((n, FOX_WIDTH), BF16),
        scratch_shapes=[pltpu.VMEM((2, ATT_TILE, ATT_TILE), F32)] * 2
                       + [pltpu.VMEM((2, ATT_TILE, 1), F32), pltpu.VMEM((2, ATT_TILE, LANES), F32)],
        compiler_params=pltpu.CompilerParams(dimension_semantics=("arbitrary",) * 2, vmem_limit_bytes=VMEM_LIMIT),
        name="fox_attn",
    )(q, k0, k1, v0, v1, g_fox)


def _mem_kv_kernel(m_ref, g_ref, w_ref, gk_ref, k_ref, v_ref):
    h = _rms(m_ref[...], g_ref[...]).astype(BF16)
    kv = jnp.dot(h, w_ref[...], preferred_element_type=F32)
    gk = gk_ref[...]
    ks = [_rms(kv[:, a * XA_HEAD_DIM:(a + 1) * XA_HEAD_DIM], gk) for a in range(XA_HEADS)]
    k_ref[...] = jnp.concatenate(ks, axis=1).astype(BF16)
    v_ref[...] = kv[:, XA_WIDTH:].astype(BF16)


def _mem_kv(mem, g, w_xkv, g_xk):
    n = mem.shape[0]
    tm = ROW_TILE
    row = lambda width: pl.BlockSpec((tm, width), lambda i: (i, 0))
    return pl.pallas_call(
        _mem_kv_kernel,
        grid=(n // tm,),
        in_specs=[row(D_MODEL), _const_spec((1, D_MODEL)), _const_spec(w_xkv.shape), _const_spec((1, XA_HEAD_DIM))],
        out_specs=[row(XA_WIDTH)] * 2,
        out_shape=[jax.ShapeDtypeStruct((n, XA_WIDTH), BF16)] * 2,
        compiler_params=pltpu.CompilerParams(dimension_semantics=("arbitrary",), vmem_limit_bytes=VMEM_LIMIT),
        name="mem_kv",
    )(mem, g, w_xkv, g_xk)


def _post_kernel(x_ref, yc_ref, yf_ref, wo_ref, g_ref, wq_ref, gq_ref, k_ref, v_ref, wxo_ref, *rest):
    o_ref = rest[-4] if len(rest) > 1 else rest[0]
    x1 = (x_ref[...]
          + jnp.dot(yc_ref[...], wo_ref[0:CONV_CH, :], preferred_element_type=F32)
          + jnp.dot(yf_ref[...], wo_ref[CONV_CH:, :], preferred_element_type=F32))
    hx = _rms(x1, g_ref[...]).astype(BF16)
    qx = jnp.dot(hx, wq_ref[...], preferred_element_type=F32)
    gq = gq_ref[...]
    heads = []
    for a in range(XA_HEADS):
        sl = slice(a * XA_HEAD_DIM, (a + 1) * XA_HEAD_DIM)
        qa = _rms(qx[:, sl], gq).astype(BF16)
        s = lax.dot_general(qa, k_ref[:, sl], (((1,), (1,)), ((), ())),
                            preferred_element_type=F32) * (XA_HEAD_DIM ** -0.5)
        e = jnp.exp(s - jnp.max(s, axis=1, keepdims=True))
        p = (e / jnp.sum(e, axis=1, keepdims=True)).astype(BF16)
        heads.append(jnp.dot(p, v_ref[:, sl], preferred_element_type=F32).astype(BF16))
    att = jnp.concatenate(heads, axis=1)
    x2 = x1 + jnp.dot(att, wxo_ref[...], preferred_element_type=F32)
    o_ref[...] = x2
    if len(rest) > 1:
        _route_tile(x2, *rest[:-4], *rest[-3:])


def _post(x, yc, yf, w_out, g_xa, w_xq, g_xq, kx, vx, w_xo, seq, n_mem, router=()):
    n = x.shape[0]
    tm = ROW_TILE
    tiles_per_seq = seq // tm
    row = lambda width: pl.BlockSpec((tm, width), lambda i: (i, 0))
    memspec = pl.BlockSpec((n_mem, XA_WIDTH), lambda i: (i // tiles_per_seq, 0))
    out_specs, out_shape = [row(D_MODEL)], [jax.ShapeDtypeStruct((n, D_MODEL), F32)]
    if router:
        out_specs += [row(D_MODEL), row(LANES), pl.BlockSpec((None, 1, LANES), lambda i: (i, 0, 0))]
        out_shape += [jax.ShapeDtypeStruct((n, D_MODEL), BF16), jax.ShapeDtypeStruct((n, LANES), F32),
                      jax.ShapeDtypeStruct((n // tm, 1, LANES), F32)]
    return pl.pallas_call(
        _post_kernel,
        grid=(n // tm,),
        in_specs=[row(D_MODEL), row(CONV_CH), row(FOX_WIDTH), _const_spec(w_out.shape), _const_spec((1, D_MODEL)),
                  _const_spec(w_xq.shape), _const_spec((1, XA_HEAD_DIM)), memspec, memspec, _const_spec(w_xo.shape)]
                 + [_const_spec(a.shape) for a in router],
        out_specs=out_specs,
        out_shape=out_shape,
        compiler_params=pltpu.CompilerParams(dimension_semantics=("arbitrary",), vmem_limit_bytes=VMEM_LIMIT),
        name="post",
    )(x, yc, yf, w_out, g_xa, w_xq, g_xq, kx, vx, w_xo, *router)


def _swiglu_acc(h, wg_ref, wu_ref, wd_ref):
    acc = None
    for c in range(D_FF // FF_CHUNK):
        sl = slice(c * FF_CHUNK, (c + 1) * FF_CHUNK)
        gate = jnp.dot(h, wg_ref[:, sl], preferred_element_type=F32)
        up = jnp.dot(h, wu_ref[:, sl], preferred_element_type=F32)
        a = (gate * (1.0 / (1.0 + jnp.exp(-gate))) * up).astype(BF16)
        part = jnp.dot(a, wd_ref[sl, :], preferred_element_type=F32)
        acc = part if acc is None else acc + part
    return acc


def _ffn_kernel(x_ref, g_ref, wg_ref, wu_ref, wd_ref, o_ref):
    x = x_ref[...]
    h = _rms(x, g_ref[...]).astype(BF16)
    o_ref[...] = x + _swiglu_acc(h, wg_ref, wu_ref, wd_ref)


def _ffn(x, g, wg, wu, wd):
    n = x.shape[0]
    tm = ROW_TILE
    row = pl.BlockSpec((tm, D_MODEL), lambda i: (i, 0))
    return pl.pallas_call(
        _ffn_kernel,
        grid=(n // tm,),
        in_specs=[row, _const_spec((1, D_MODEL)), _const_spec(wg.shape), _const_spec(wu.shape),
                  _const_spec(wd.shape)],
        out_specs=row,
        out_shape=jax.ShapeDtypeStruct((n, D_MODEL), F32),
        compiler_params=pltpu.CompilerParams(dimension_semantics=("arbitrary",), vmem_limit_bytes=VMEM_LIMIT),
        name="ffn",
    )(x, g, wg, wu, wd)


def _route_tile(x, g_ref, wr_hi_ref, wr_lo_ref, br_ref, tri_ref, h_ref, meta_ref, cnt_ref):
    lane = lax.broadcasted_iota(jnp.int32, meta_ref.shape, 1).astype(F32)
    hf = _rms(x, g_ref[...])
    h_hi, h_lo = _split2(hf)
    h_ref[...] = h_hi
    logits = (jnp.dot(h_hi, wr_hi_ref[...], preferred_element_type=F32)
              + jnp.dot(h_hi, wr_lo_ref[...], preferred_element_type=F32)
              + jnp.dot(h_lo, wr_hi_ref[...], preferred_element_type=F32)) + br_ref[...]
    logits = jnp.where(lane < N_EXPERTS, logits, -jnp.inf)
    m1 = jnp.max(logits, axis=1, keepdims=True)
    i1 = jnp.min(jnp.where(logits == m1, lane, LANES), axis=1, keepdims=True)
    rest = jnp.where(lane == i1, -jnp.inf, logits)
    m2 = jnp.max(rest, axis=1, keepdims=True)
    i2 = jnp.min(jnp.where(rest == m2, lane, LANES), axis=1, keepdims=True)
    e2 = jnp.exp(m2 - m1)
    w1 = 1.0 / (1.0 + e2)
    w2 = e2 * w1
    onehot = jnp.where((lane == i1) | (lane == i2), 1.0, 0.0)
    ranks = jnp.dot(tri_ref[...], onehot.astype(BF16), preferred_element_type=F32) - onehot
    lr1 = jnp.sum(jnp.where(lane == i1, ranks, 0.0), axis=1, keepdims=True)
    lr2 = jnp.sum(jnp.where(lane == i2, ranks, 0.0), axis=1, keepdims=True)

    def split3(w, first_lane):
        a1 = w.astype(BF16).astype(F32)
        a2 = (w - a1).astype(BF16).astype(F32)
        a3 = (w - a1 - a2).astype(BF16).astype(F32)
        return jnp.where(lane == first_lane, a1, jnp.where(lane == first_lane + 1, a2,
                         jnp.where(lane == first_lane + 2, a3, 0.0)))

    meta = jnp.where(lane == META_EXPERT, i1, jnp.where(lane == META_EXPERT + 1, i2, 0.0))
    meta = meta + jnp.where(lane == META_RANK, lr1, jnp.where(lane == META_RANK + 1, lr2, 0.0))
    meta_ref[...] = meta + split3(w1, META_GATE[0]) + split3(w2, META_GATE[1])
    cnt_ref[...] = jnp.sum(onehot, axis=0, keepdims=True)


def _routing_tables(counts, n_exp_tiles):
    cnt = counts[:, 0, :N_EXPERTS].astype(jnp.int32)
    seg = (cnt + SEG_ALIGN - 1) // SEG_ALIGN * SEG_ALIGN
    loff = jnp.cumsum(seg, axis=1) - seg
    tot = jnp.sum(seg, axis=0)
    gsize = (tot + EXP_TILE - 1) // EXP_TILE * EXP_TILE
    gend = jnp.cumsum(gsize)
    gstart = gend - gsize
    sstart = gstart[None, :] + jnp.cumsum(seg, axis=0) - seg
    n_used = gend[-1] // EXP_TILE
    tile = jnp.arange(n_exp_tiles, dtype=jnp.int32)
    expert = jnp.sum((tile[:, None] * EXP_TILE >= gend[None, :]).astype(jnp.int32), axis=1)
    expert = jnp.where(tile < n_used, expert, expert[n_used - 1])
    flat = lambda a: a.reshape(-1).astype(jnp.int32)
    return dict(seg=flat(seg), loff=flat(loff), sstart=flat(sstart),
                pad_start=flat(jnp.concatenate([gstart + tot, gend[-1:]])),
                pad_len=flat(gsize - tot), expert=flat(expert), n_used=flat(n_used))


def _chunks(n_rows, bits):
    n = n_rows // SEG_ALIGN
    for b in range(bits):
        yield ((n >> b) & 1) == 1, (n & ((1 << b) - 1)) * SEG_ALIGN, SEG_ALIGN << b


def _segment_copies(i, seg_s, loff_s, sstart_s, make):
    for e in range(N_EXPERTS):
        k = i * N_EXPERTS + e
        for cond, off, size in _chunks(seg_s[k], SEG_BITS):
            yield cond, make(pl.multiple_of(loff_s[k] + off, SEG_ALIGN),
                             pl.multiple_of(sstart_s[k] + off, SEG_ALIGN), size)


def _start_copies(plan):
    for cond, copies in plan:
        @pl.when(cond)
        def _():
            for c in copies:
                c.start()


def _wait_copies(plan):
    for cond, copies in plan:
        @pl.when(cond)
        def _():
            for c in copies:
                c.wait()


def _local_slot(expert, rank, loff_s, i):
    off = jnp.zeros_like(rank)
    for e in range(N_EXPERTS):
        off = jnp.where(expert == e, loff_s[i * N_EXPERTS + e].astype(F32), off)
    return (off + rank).astype(jnp.int32)


def _dispatch_kernel(seg_s, loff_s, sstart_s, pstart_s, plen_s, h_ref, meta_ref, xs_hbm, gs_hbm,
                     xloc, gloc, zx, zg, sem):
    i = pl.program_id(0)
    last = pl.num_programs(0) - 1
    tm = h_ref.shape[0]
    buf = i % 2
    meta = meta_ref[...]
    meta_t = meta.T
    slot = lax.broadcasted_iota(jnp.int32, (LOCAL_SLOTS, tm), 0)
    p1, p2 = (slot == _local_slot(meta_t[META_EXPERT + c:META_EXPERT + c + 1, :],
                                  meta_t[META_RANK + c:META_RANK + c + 1, :], loff_s, i) for c in range(2))
    xloc[buf] = jnp.dot(jnp.where(p1 | p2, 1.0, 0.0).astype(BF16), h_ref[...], preferred_element_type=F32)
    lane = lax.broadcasted_iota(jnp.int32, meta.shape, 1)
    gates = [jnp.where((lane >= first) & (lane < first + 3), meta, 0.0).astype(BF16) for first in META_GATE]
    gloc[buf] = (jnp.dot(jnp.where(p1, 1.0, 0.0).astype(BF16), gates[0], preferred_element_type=F32)
                 + jnp.dot(jnp.where(p2, 1.0, 0.0).astype(BF16), gates[1], preferred_element_type=F32))

    def plan(tile, b):
        def make(local, dst, size):
            return [pltpu.make_async_copy(xloc.at[b, pl.ds(local, size)], xs_hbm.at[pl.ds(dst, size)], sem.at[b, 0]),
                    pltpu.make_async_copy(gloc.at[b, pl.ds(local, size)], gs_hbm.at[pl.ds(dst, size)], sem.at[b, 1])]
        return list(_segment_copies(tile, seg_s, loff_s, sstart_s, make))

    _start_copies(plan(i, buf))

    @pl.when(i > 0)
    def _():
        _wait_copies(plan(i - 1, 1 - buf))

    @pl.when(i == last)
    def _():
        _wait_copies(plan(i, buf))
        zx[...] = jnp.zeros_like(zx)
        zg[...] = jnp.zeros_like(zg)
        fill = []
        for e in range(N_EXPERTS):
            for cond, off, size in _chunks(plen_s[e], PAD_BITS):
                dst = pl.multiple_of(pstart_s[e] + off, SEG_ALIGN)
                fill.append((cond, [
                    pltpu.make_async_copy(zx.at[pl.ds(0, size)], xs_hbm.at[pl.ds(dst, size)], sem.at[0, 0]),
                    pltpu.make_async_copy(zg.at[pl.ds(0, size)], gs_hbm.at[pl.ds(dst, size)], sem.at[0, 1])]))
        _start_copies(fill)
        _wait_copies(fill)

        pad_rows = zx.shape[0]
        tail = pstart_s[N_EXPERTS]

        def zero_chunk(c, _):
            dst = pl.multiple_of(tail + c * pad_rows, pad_rows)
            copies = [pltpu.make_async_copy(zx, xs_hbm.at[pl.ds(dst, pad_rows)], sem.at[0, 0]),
                      pltpu.make_async_copy(zg, gs_hbm.at[pl.ds(dst, pad_rows)], sem.at[0, 1])]
            for copy in copies:
                copy.start()
            for copy in copies:
                copy.wait()
            return 0

        lax.fori_loop(0, (xs_hbm.shape[0] - tail) // pad_rows, zero_chunk, 0)


def _dispatch(h, meta, tables, n_slots):
    n = h.shape[0]
    tm = ROW_TILE
    row = lambda width: pl.BlockSpec((tm, width), lambda i, *_: (i, 0))
    pad_rows = SEG_ALIGN << (PAD_BITS - 1)
    return pl.pallas_call(
        _dispatch_kernel,
        grid_spec=pltpu.PrefetchScalarGridSpec(
            num_scalar_prefetch=5, grid=(n // tm,),
            in_specs=[row(D_MODEL), row(LANES)],
            out_specs=[pl.BlockSpec(memory_space=pl.ANY)] * 2,
            scratch_shapes=[pltpu.VMEM((2, LOCAL_SLOTS, D_MODEL), F32), pltpu.VMEM((2, LOCAL_SLOTS, LANES), F32),
                            pltpu.VMEM((pad_rows, D_MODEL), F32), pltpu.VMEM((pad_rows, LANES), F32),
                            pltpu.SemaphoreType.DMA((2, 2))]),
        out_shape=[jax.ShapeDtypeStruct((n_slots, D_MODEL), F32), jax.ShapeDtypeStruct((n_slots, LANES), F32)],
        compiler_params=pltpu.CompilerParams(dimension_semantics=("arbitrary",), vmem_limit_bytes=VMEM_LIMIT),
        name="dispatch",
    )(tables["seg"], tables["loff"], tables["sstart"], tables["pad_start"], tables["pad_len"], h, meta)


def _expert_kernel(expert_s, n_used_s, x_ref, g_ref, wg_ref, wu_ref, wd_ref, y_ref):
    used = pl.program_id(0) < n_used_s[0]

    @pl.when(used)
    def _():
        gate = jnp.sum(g_ref[...], axis=1, keepdims=True)
        y_ref[...] = gate * _swiglu_acc(x_ref[...].astype(BF16), wg_ref, wu_ref, wd_ref)

    @pl.when(jnp.logical_not(used))
    def _():
        y_ref[...] = jnp.zeros_like(y_ref)


def _experts(xs, gs, tables, wg, wu, wd):
    n_slots = xs.shape[0]
    used = lambda i, expert, n_used: (jnp.minimum(i, n_used[0] - 1), 0)
    weight = lambda shape: pl.BlockSpec((None,) + shape, lambda i, expert, n_used: (expert[i], 0, 0))
    return pl.pallas_call(
        _expert_kernel,
        grid_spec=pltpu.PrefetchScalarGridSpec(
            num_scalar_prefetch=2, grid=(n_slots // EXP_TILE,),
            in_specs=[pl.BlockSpec((EXP_TILE, D_MODEL), used), pl.BlockSpec((EXP_TILE, LANES), used),
                      weight((D_MODEL, D_FF)), weight((D_MODEL, D_FF)), weight((D_FF, D_MODEL))],
            out_specs=pl.BlockSpec((EXP_TILE, D_MODEL), lambda i, expert, n_used: (i, 0))),
        out_shape=jax.ShapeDtypeStruct((n_slots, D_MODEL), F32),
        compiler_params=pltpu.CompilerParams(dimension_semantics=("arbitrary",), vmem_limit_bytes=VMEM_LIMIT),
        name="experts",
    )(tables["expert"], tables["n_used"], xs, gs, wg, wu, wd)


def _combine_kernel(seg_s, loff_s, sstart_s, x_ref, meta_ref, y_hbm, o_ref, yloc, sem):
    i = pl.program_id(0)
    tm = x_ref.shape[0]
    buf = i % 2

    def plan(tile, b):
        def make(local, src, size):
            return [pltpu.make_async_copy(y_hbm.at[pl.ds(src, size)], yloc.at[b, pl.ds(local, size)], sem.at[b])]
        return list(_segment_copies(tile, seg_s, loff_s, sstart_s, make))

    @pl.when(i == 0)
    def _():
        yloc[...] = jnp.zeros_like(yloc)
        _start_copies(plan(0, 0))

    @pl.when(i + 1 < pl.num_programs(0))
    def _():
        _start_copies(plan(i + 1, 1 - buf))

    meta = meta_ref[...]
    slot = lax.broadcasted_iota(jnp.int32, (tm, LOCAL_SLOTS), 1)
    p1, p2 = (slot == _local_slot(meta[:, META_EXPERT + c:META_EXPERT + c + 1],
                                  meta[:, META_RANK + c:META_RANK + c + 1], loff_s, i) for c in range(2))
    pick = jnp.where(p1 | p2, 1.0, 0.0).astype(BF16)
    _wait_copies(plan(i, buf))
    hi, lo = _split2(yloc[buf])
    o_ref[...] = (x_ref[...] + jnp.dot(pick, hi, preferred_element_type=F32)
                  + jnp.dot(pick, lo, preferred_element_type=F32))


def _combine(x, meta, ys, tables):
    n = x.shape[0]
    tm = ROW_TILE
    row = lambda width: pl.BlockSpec((tm, width), lambda i, *_: (i, 0))
    return pl.pallas_call(
        _combine_kernel,
        grid_spec=pltpu.PrefetchScalarGridSpec(
            num_scalar_prefetch=3, grid=(n // tm,),
            in_specs=[row(D_MODEL), row(LANES), pl.BlockSpec(memory_space=pl.ANY)],
            out_specs=row(D_MODEL),
            scratch_shapes=[pltpu.VMEM((2, LOCAL_SLOTS, D_MODEL), F32), pltpu.SemaphoreType.DMA((2,))]),
        out_shape=jax.ShapeDtypeStruct((n, D_MODEL), F32),
        compiler_params=pltpu.CompilerParams(dimension_semantics=("arbitrary",), vmem_limit_bytes=VMEM_LIMIT),
        name="combine",
    )(tables["seg"], tables["loff"], tables["sstart"], x, meta, ys)


def _moe(x, h, meta, counts, wg, wu, wd):
    n = x.shape[0]
    n_tiles = n // ROW_TILE
    n_slots = 2 * n + n_tiles * N_EXPERTS * (SEG_ALIGN - 1) + N_EXPERTS * EXP_TILE
    n_slots = -(-n_slots // EXP_TILE) * EXP_TILE
    tables = _routing_tables(counts, n_slots // EXP_TILE)
    xs, gs = _dispatch(h, meta, tables, n_slots)
    ys = _experts(xs, gs, tables, wg, wu, wd)
    return _combine(x, meta, ys, tables)


def _pad_lanes(a, width=LANES):
    return jnp.pad(a, ((0, 0), (0, width - a.shape[1])))


def kernel(x, mem, g_mix, w_in, conv_w, b_f, g_q, g_k, g_conv_out, g_fox_out, w_out, g_xa, g_mem, w_xq, w_xkv,
           g_xq, g_xk, w_xo, g_ffn, w_gate, w_up, w_down, w_router, b_router, we_gate, we_up, we_down):
    batch, seq, _ = x.shape
    n_mem = mem.shape[1]
    xs = x.reshape(batch * seq, D_MODEL)
    mems = mem.reshape(batch * n_mem, D_MODEL)
    n_main = 3 * CONV_CH + 3 * FOX_WIDTH

    group = jnp.arange(MXU_DIM) // HEAD_DIM
    gmat = (group[:, None] == group[None, :]).astype(BF16)
    tri = jnp.tril(jnp.ones((ROW_TILE, ROW_TILE), BF16))
    place = _aug_placement()
    row1 = lambda a: a.reshape(1, -1)

    for l in range(DEPTH):
        w_f = _pad_lanes(jnp.tile(w_in[l][:, n_main:], (1, 3))).astype(BF16)
        b_f3 = _pad_lanes(jnp.tile(row1(b_f[l]), (1, 3)))
        yc, q, k0, k1, v0, v1 = _mix_in(
            xs, row1(g_mix[l]), w_in[l][:, :n_main].astype(BF16), w_f, conv_w[l], b_f3,
            jnp.tile(row1(g_q[l]), (1, FOX_HEADS)), jnp.tile(row1(g_k[l]), (1, FOX_HEADS)),
            row1(g_conv_out[l]), gmat, tri, place, seq)
        yf = _fox_attention(q, k0, k1, v0, v1, row1(g_fox_out[l]), batch, seq)
        kx, vx = _mem_kv(mems, row1(g_mem[l]), w_xkv[l].astype(BF16), row1(g_xk[l]))
        i = l // 2
        router = ()
        if l % 2 == 1:
            wr = _pad_lanes(w_router[i])
            wr_hi = wr.astype(BF16)
            wr_lo = (wr - wr_hi.astype(F32)).astype(BF16)
            router = (row1(g_ffn[l]), wr_hi, wr_lo, _pad_lanes(row1(b_router[i])), tri)
        res = _post(xs, yc, yf, w_out[l].astype(BF16), row1(g_xa[l]), w_xq[l].astype(BF16), row1(g_xq[l]),
                    kx, vx, w_xo[l].astype(BF16), seq, n_mem, router)
        if l % 2 == 0:
            xs = _ffn(res[0], row1(g_ffn[l]), w_gate[i].astype(BF16), w_up[i].astype(BF16), w_down[i].astype(BF16))
        else:
            xs = _moe(*res, we_gate[i].astype(BF16), we_up[i].astype(BF16), we_down[i].astype(BF16))
    return xs.reshape(batch, seq, D_MODEL)
```

```python
import functools

import jax
import jax.numpy as jnp
import numpy as np
from jax import lax
from jax.experimental import pallas as pl
from jax.experimental.pallas import tpu as pltpu

D_MODEL = 1024
DEPTH = 4
HEAD_DIM = 64
CONV_CH = 512
CONV_K = 3
FOX_HEADS = 8
FOX_WIDTH = 512
XA_HEADS = 4
XA_HEAD_DIM = 128
XA_WIDTH = 512
D_FF = 2816
N_EXPERTS = 8
EPS = 1e-6

LANES = 128
SUBLANES = 8
MXU_DIM = 256
VMEM_LIMIT = 56 * 1024 * 1024

ROW_TILE = 512
ATT_TILE = 1024
FF_CHUNK = 256
EXP_TILE = 512
SEG_ALIGN = SUBLANES
SEG_BITS = (ROW_TILE // SEG_ALIGN).bit_length()
PAD_BITS = (EXP_TILE // SEG_ALIGN - 1).bit_length()
LOCAL_SLOTS = -(-(2 * ROW_TILE + N_EXPERTS * (SEG_ALIGN - 1)) // LANES) * LANES
META_EXPERT, META_RANK, META_GATE = 0, 4, (8, 16)

F32 = jnp.float32
BF16 = jnp.bfloat16


def _const_spec(shape):
    return pl.BlockSpec(shape, lambda *_: (0,) * len(shape), pipeline_mode=pl.Buffered(1))


def _rms(x, g):
    return x * lax.rsqrt(jnp.mean(x * x, axis=-1, keepdims=True) + EPS) * g


def _split2(x):
    hi = x.astype(BF16)
    lo = (x - hi.astype(F32)).astype(BF16)
    return hi, lo


def _group_rms(y, g, gmat):
    sq = (y * y).astype(BF16)
    parts = [jnp.dot(sq[:, c * MXU_DIM:(c + 1) * MXU_DIM], gmat, preferred_element_type=F32)
             for c in range(y.shape[1] // MXU_DIM)]
    ms = jnp.concatenate(parts, axis=1) * (1.0 / HEAD_DIM)
    return y * lax.rsqrt(ms + EPS) * g


def _mix_in_kernel(tiles_per_seq, x_ref, g_ref, w_ref, wf_ref, cw_ref, bf_ref, gq_ref, gk_ref, gc_ref,
                   gmat_ref, tri_ref, place_ref, yc_ref, q_ref, k0_ref, k1_ref, v0_ref, v1_ref, ubuf, ccarry):
    i = pl.program_id(0)
    tm = x_ref.shape[0]
    h = _rms(x_ref[...], g_ref[...]).astype(BF16)

    def proj(j):
        return jnp.dot(h, w_ref[:, j * CONV_CH:(j + 1) * CONV_CH], preferred_element_type=F32)

    @pl.when(i % tiles_per_seq == 0)
    def _():
        ubuf[0:SUBLANES, :] = jnp.zeros((SUBLANES, CONV_CH), F32)
        ccarry[...] = jnp.zeros_like(ccarry)

    ubuf[SUBLANES:SUBLANES + tm, :] = proj(1) * proj(2)
    y = (cw_ref[0:1, :] * ubuf[SUBLANES - 2:SUBLANES - 2 + tm, :]
         + cw_ref[1:2, :] * ubuf[SUBLANES - 1:SUBLANES - 1 + tm, :]
         + cw_ref[2:3, :] * ubuf[SUBLANES:SUBLANES + tm, :])
    ubuf[0:SUBLANES, :] = ubuf[tm:tm + SUBLANES, :]
    gmat = gmat_ref[...]
    yc_ref[...] = _group_rms(proj(0) * y, gc_ref[...], gmat).astype(BF16)

    q_ref[...] = (_group_rms(proj(3), gq_ref[...], gmat) * (HEAD_DIM ** -0.5)).astype(BF16)

    z = jnp.dot(h, wf_ref[...], preferred_element_type=F32) + bf_ref[...]
    logf = jnp.minimum(z, 0.0) - jnp.log1p(jnp.exp(-jnp.abs(z)))
    p1 = logf.astype(BF16)
    r1 = logf - p1.astype(F32)
    p2 = r1.astype(BF16)
    p3 = (r1 - p2.astype(F32)).astype(BF16)
    lane = lax.broadcasted_iota(jnp.int32, logf.shape, 1)
    parts = jnp.where(lane < FOX_HEADS, p1, jnp.where(lane < 2 * FOX_HEADS, p2, p3))
    cs = jnp.dot(tri_ref[...], parts, preferred_element_type=F32)
    cs = cs + pltpu.roll(cs, LANES - FOX_HEADS, axis=1) + pltpu.roll(cs, LANES - 2 * FOX_HEADS, axis=1)
    c = jnp.where(lane < FOX_HEADS, cs + ccarry[...], 0.0)
    ccarry[...] = c[tm - 1:tm, :]

    c3 = -(c + pltpu.roll(c, FOX_HEADS, axis=1) + pltpu.roll(c, 2 * FOX_HEADS, axis=1))
    n1 = c3.astype(BF16)
    r1 = c3 - n1.astype(F32)
    n2 = r1.astype(BF16)
    n3 = (r1 - n2.astype(F32)).astype(BF16)
    nparts = jnp.where(lane < FOX_HEADS, n1, jnp.where(lane < 2 * FOX_HEADS, n2, n3))
    kn = _group_rms(proj(4), gk_ref[...], gmat)
    vv = proj(5)
    lane_w = lax.broadcasted_iota(jnp.int32, (1, FOX_WIDTH), 1)
    for j, (k_ref, v_ref) in enumerate(((k0_ref, v0_ref), (k1_ref, v1_ref))):
        own = (lane_w & HEAD_DIM) == j * HEAD_DIM
        aug = jnp.dot(nparts, place_ref[j], preferred_element_type=F32)
        k_ref[...] = jnp.where(own, kn, aug).astype(BF16)
        ones_lane = (lane_w & (LANES - 1)) == (1 - j) * HEAD_DIM
        v_ref[...] = jnp.where(own, vv, jnp.where(ones_lane, 1.0, 0.0)).astype(BF16)


def _aug_placement():
    place = np.zeros((2, LANES, FOX_WIDTH), np.float32)
    for head in range(FOX_HEADS):
        pair, j = divmod(head, 2)
        for m in range(3):
            place[j, FOX_HEADS * m + head, pair * LANES + (1 - j) * HEAD_DIM + m] = 1.0
    return jnp.asarray(place, BF16)


def _mix_in(x, g, w_main, w_f, conv_w, b_f, gq, gk, gc, gmat, tri, place, seq):
    n = x.shape[0]
    tm = ROW_TILE
    row = lambda width: pl.BlockSpec((tm, width), lambda i: (i, 0))
    return pl.pallas_call(
        functools.partial(_mix_in_kernel, seq // tm),
        grid=(n // tm,),
        in_specs=[row(D_MODEL), _const_spec((1, D_MODEL)), _const_spec(w_main.shape), _const_spec(w_f.shape),
                  _const_spec(conv_w.shape), _const_spec((1, LANES)), _const_spec((1, CONV_CH)),
                  _const_spec((1, CONV_CH)), _const_spec((1, CONV_CH)), _const_spec(gmat.shape),
                  _const_spec(tri.shape), _const_spec(place.shape)],
        out_specs=[row(CONV_CH)] * 6,
        out_shape=[jax.ShapeDtypeStruct((n, CONV_CH), BF16)] * 6,
        scratch_shapes=[pltpu.VMEM((tm + SUBLANES, CONV_CH), F32), pltpu.VMEM((1, LANES), F32)],
        compiler_params=pltpu.CompilerParams(dimension_semantics=("arbitrary",), vmem_limit_bytes=VMEM_LIMIT),
        name="mix_in",
    )(x, g, w_main, w_f, conv_w, b_f, gq, gk, gc, gmat, tri, place)


def _fox_kernel(q_ref, k0_ref, k1_ref, v0_ref, v1_ref, g_ref, o_ref, sa_scr, sb_scr, m_scr, acc_scr):
    tq = tk = ATT_TILE
    nq = q_ref.shape[0] // tq
    k_refs = (k0_ref, k1_ref)
    v_refs = (v0_ref, v1_ref)
    bufs = (sa_scr, sb_scr)
    lane = lax.broadcasted_iota(jnp.int32, (1, LANES), 1)
    own = [(lane & HEAD_DIM) == j * HEAD_DIM for j in range(2)]
    spare = [(1 - j) * HEAD_DIM for j in range(2)]
    ones = [jnp.where((lane >= spare[j]) & (lane < spare[j] + 3), 1.0, 0.0).astype(BF16) for j in range(2)]
    causal = lax.broadcasted_iota(jnp.int32, (tq, tk), 0) >= lax.broadcasted_iota(jnp.int32, (tq, tk), 1)
    gain = g_ref[...]

    def scores(qi, kv, s_scr):
        q = q_ref[qi * tq:(qi + 1) * tq, :]
        start = kv * tk if isinstance(kv, int) else pl.multiple_of(kv * tk, tk)
        for j in range(2):
            s_scr[j] = lax.dot_general(jnp.where(own[j], q, ones[j]), k_refs[j][pl.ds(start, tk), :],
                                       (((1,), (1,)), ((), ())), preferred_element_type=F32)

    def absorb(kv, s_scr, carry, on_diagonal=False):
        start = kv * tk if isinstance(kv, int) else pl.multiple_of(kv * tk, tk)
        new = []
        for j in range(2):
            m, acc = carry[j]
            s = s_scr[j]
            if on_diagonal:
                s = jnp.where(causal, s, -jnp.inf)
            m_new = jnp.maximum(m, jnp.max(s, axis=1, keepdims=True))
            p = jnp.exp(s - m_new).astype(BF16)
            acc = jnp.exp(m - m_new) * acc + jnp.dot(p, v_refs[j][pl.ds(start, tk), :], preferred_element_type=F32)
            new.append((m_new, acc))
        return tuple(new)

    def load_state():
        return tuple((m_scr[j], acc_scr[j]) for j in range(2))

    def store_state(carry):
        for j in range(2):
            m_scr[j], acc_scr[j] = carry[j]

    def finish(qi, carry):
        out = None
        for j in range(2):
            acc = carry[j][1]
            l = jnp.sum(jnp.where(lane == spare[j], acc, 0.0), axis=1, keepdims=True)
            o = jnp.where(own[j], acc, 0.0)
            o = o * lax.rsqrt(jnp.sum(o * o, axis=1, keepdims=True) * (1.0 / HEAD_DIM) + EPS * (l * l))
            out = o if out is None else out + o
        o_ref[qi * tq:(qi + 1) * tq, :] = (out * gain).astype(BF16)

    init = tuple((jnp.full((tq, 1), -jnp.inf, F32), jnp.zeros((tq, LANES), F32)) for _ in range(2))
    t0 = 0
    scores(0, 0, bufs[0])
    for qi in range(nq):
        first, second = bufs[t0 % 2], bufs[(t0 + 1) % 2]

        def two_blocks(kv, carry, qi=qi, first=first, second=second):
            scores(qi, kv + 1, second)
            carry = absorb(kv, first, carry)
            scores(qi, kv + 2, first)
            return absorb(kv + 1, second, carry)

        def two_blocks_in_place(u, _, two_blocks=two_blocks):
            store_state(two_blocks(2 * u, load_state()))
            return 0

        carry = init
        if qi // 2 == 1:
            carry = two_blocks(0, carry)
        elif qi // 2 > 1:
            store_state(carry)
            lax.fori_loop(0, qi // 2, two_blocks_in_place, 0)
            carry = load_state()
        if qi % 2 == 1:
            scores(qi, qi, second)
            carry = absorb(qi - 1, first, carry)
            diagonal, free = second, first
        else:
            diagonal, free = first, second
        if qi + 1 < nq:
            scores(qi + 1, 0, free)
        finish(qi, absorb(qi, diagonal, carry, on_diagonal=True))
        t0 += qi + 1


def _fox_attention(q, k0, k1, v0, v1, g_fox, batch, seq):
    n = q.shape[0]
    spec = pl.BlockSpec((seq, LANES), lambda b, p: (b, p))
    return pl.pallas_call(
        _fox_kernel,
        grid=(batch, FOX_HEADS // 2),
        in_specs=[spec] * 5 + [pl.BlockSpec((1, LANES), lambda b, p: (0, p))],
        out_specs=spec,
        out_shape=jax.ShapeDtypeStruct((n, FOX_WIDTH), BF16),
        scratch_shapes=[pltpu.VMEM((2, ATT_TILE, ATT_TILE), F32)] * 2
                       + [pltpu.VMEM((2, ATT_TILE, 1), F32), pltpu.VMEM((2, ATT_TILE, LANES), F32)],
        compiler_params=pltpu.CompilerParams(dimension_semantics=("arbitrary",) * 2, vmem_limit_bytes=VMEM_LIMIT),
        name="fox_attn",
    )(q, k0, k1, v0, v1, g_fox)


def _mem_kv_kernel(m_ref, g_ref, w_ref, gk_ref, k_ref, v_ref):
    h = _rms(m_ref[...], g_ref[...]).astype(BF16)
    kv = jnp.dot(h, w_ref[...], preferred_element_type=F32)
    gk = gk_ref[...]
    ks = [_rms(kv[:, a * XA_HEAD_DIM:(a + 1) * XA_HEAD_DIM], gk) for a in range(XA_HEADS)]
    k_ref[...] = jnp.concatenate(ks, axis=1).astype(BF16)
    v_ref[...] = kv[:, XA_WIDTH:].astype(BF16)


def _mem_kv(mem, g, w_xkv, g_xk):
    n = mem.shape[0]
    tm = ROW_TILE
    row = lambda width: pl.BlockSpec((tm, width), lambda i: (i, 0))
    return pl.pallas_call(
        _mem_kv_kernel,
        grid=(n // tm,),
        in_specs=[row(D_MODEL), _const_spec((1, D_MODEL)), _const_spec(w_xkv.shape), _const_spec((1, XA_HEAD_DIM))],
        out_specs=[row(XA_WIDTH)] * 2,
        out_shape=[jax.ShapeDtypeStruct((n, XA_WIDTH), BF16)] * 2,
        compiler_params=pltpu.CompilerParams(dimension_semantics=("arbitrary",), vmem_limit_bytes=VMEM_LIMIT),
        name="mem_kv",
    )(mem, g, w_xkv, g_xk)


def _post_kernel(x_ref, yc_ref, yf_ref, wo_ref, g_ref, wq_ref, gq_ref, k_ref, v_ref, wxo_ref, *rest):
    o_ref = rest[-4] if len(rest) > 1 else rest[0]
    x1 = (x_ref[...]
          + jnp.dot(yc_ref[...], wo_ref[0:CONV_CH, :], preferred_element_type=F32)
          + jnp.dot(yf_ref[...], wo_ref[CONV_CH:, :], preferred_element_type=F32))
    hx = _rms(x1, g_ref[...]).astype(BF16)
    qx = jnp.dot(hx, wq_ref[...], preferred_element_type=F32)
    gq = gq_ref[...]
    heads = []
    for a in range(XA_HEADS):
        sl = slice(a * XA_HEAD_DIM, (a + 1) * XA_HEAD_DIM)
        qa = _rms(qx[:, sl], gq).astype(BF16)
        s = lax.dot_general(qa, k_ref[:, sl], (((1,), (1,)), ((), ())),
                            preferred_element_type=F32) * (XA_HEAD_DIM ** -0.5)
        e = jnp.exp(s - jnp.max(s, axis=1, keepdims=True))
        p = (e / jnp.sum(e, axis=1, keepdims=True)).astype(BF16)
        heads.append(jnp.dot(p, v_ref[:, sl], preferred_element_type=F32).astype(BF16))
    att = jnp.concatenate(heads, axis=1)
    x2 = x1 + jnp.dot(att, wxo_ref[...], preferred_element_type=F32)
    o_ref[...] = x2
    if len(rest) > 1:
        _route_tile(x2, *rest[:-4], *rest[-3:])


def _post(x, yc, yf, w_out, g_xa, w_xq, g_xq, kx, vx, w_xo, seq, n_mem, router=()):
    n = x.shape[0]
    tm = ROW_TILE
    tiles_per_seq = seq // tm
    row = lambda width: pl.BlockSpec((tm, width), lambda i: (i, 0))
    memspec = pl.BlockSpec((n_mem, XA_WIDTH), lambda i: (i // tiles_per_seq, 0))
    out_specs, out_shape = [row(D_MODEL)], [jax.ShapeDtypeStruct((n, D_MODEL), F32)]
    if router:
        out_specs += [row(D_MODEL), row(LANES), pl.BlockSpec((None, 1, LANES), lambda i: (i, 0, 0))]
        out_shape += [jax.ShapeDtypeStruct((n, D_MODEL), BF16), jax.ShapeDtypeStruct((n, LANES), F32),
                      jax.ShapeDtypeStruct((n // tm, 1, LANES), F32)]
    return pl.pallas_call(
        _post_kernel,
        grid=(n // tm,),
        in_specs=[row(D_MODEL), row(CONV_CH), row(FOX_WIDTH), _const_spec(w_out.shape), _const_spec((1, D_MODEL)),
                  _const_spec(w_xq.shape), _const_spec((1, XA_HEAD_DIM)), memspec, memspec, _const_spec(w_xo.shape)]
                 + [_const_spec(a.shape) for a in router],
        out_specs=out_specs,
        out_shape=out_shape,
        compiler_params=pltpu.CompilerParams(dimension_semantics=("arbitrary",), vmem_limit_bytes=VMEM_LIMIT),
        name="post",
    )(x, yc, yf, w_out, g_xa, w_xq, g_xq, kx, vx, w_xo, *router)


def _swiglu_acc(h, wg_ref, wu_ref, wd_ref):
    acc = None
    for c in range(D_FF // FF_CHUNK):
        sl = slice(c * FF_CHUNK, (c + 1) * FF_CHUNK)
        gate = jnp.dot(h, wg_ref[:, sl], preferred_element_type=F32)
        up = jnp.dot(h, wu_ref[:, sl], preferred_element_type=F32)
        a = (gate * (1.0 / (1.0 + jnp.exp(-gate))) * up).astype(BF16)
        part = jnp.dot(a, wd_ref[sl, :], preferred_element_type=F32)
        acc = part if acc is None else acc + part
    return acc


def _ffn_kernel(x_ref, g_ref, wg_ref, wu_ref, wd_ref, o_ref):
    x = x_ref[...]
    h = _rms(x, g_ref[...]).astype(BF16)
    o_ref[...] = x + _swiglu_acc(h, wg_ref, wu_ref, wd_ref)


def _ffn(x, g, wg, wu, wd):
    n = x.shape[0]
    tm = ROW_TILE
    row = pl.BlockSpec((tm, D_MODEL), lambda i: (i, 0))
    return pl.pallas_call(
        _ffn_kernel,
        grid=(n // tm,),
        in_specs=[row, _const_spec((1, D_MODEL)), _const_spec(wg.shape), _const_spec(wu.shape),
                  _const_spec(wd.shape)],
        out_specs=row,
        out_shape=jax.ShapeDtypeStruct((n, D_MODEL), F32),
        compiler_params=pltpu.CompilerParams(dimension_semantics=("arbitrary",), vmem_limit_bytes=VMEM_LIMIT),
        name="ffn",
    )(x, g, wg, wu, wd)


def _route_tile(x, g_ref, wr_hi_ref, wr_lo_ref, br_ref, tri_ref, h_ref, meta_ref, cnt_ref):
    lane = lax.broadcasted_iota(jnp.int32, meta_ref.shape, 1).astype(F32)
    hf = _rms(x, g_ref[...])
    h_hi, h_lo = _split2(hf)
    h_ref[...] = h_hi
    logits = (jnp.dot(h_hi, wr_hi_ref[...], preferred_element_type=F32)
              + jnp.dot(h_hi, wr_lo_ref[...], preferred_element_type=F32)
              + jnp.dot(h_lo, wr_hi_ref[...], preferred_element_type=F32)) + br_ref[...]
    logits = jnp.where(lane < N_EXPERTS, logits, -jnp.inf)
    m1 = jnp.max(logits, axis=1, keepdims=True)
    i1 = jnp.min(jnp.where(logits == m1, lane, LANES), axis=1, keepdims=True)
    rest = jnp.where(lane == i1, -jnp.inf, logits)
    m2 = jnp.max(rest, axis=1, keepdims=True)
    i2 = jnp.min(jnp.where(rest == m2, lane, LANES), axis=1, keepdims=True)
    e2 = jnp.exp(m2 - m1)
    w1 = 1.0 / (1.0 + e2)
    w2 = e2 * w1
    onehot = jnp.where((lane == i1) | (lane == i2), 1.0, 0.0)
    ranks = jnp.dot(tri_ref[...], onehot.astype(BF16), preferred_element_type=F32) - onehot
    lr1 = jnp.sum(jnp.where(lane == i1, ranks, 0.0), axis=1, keepdims=True)
    lr2 = jnp.sum(jnp.where(lane == i2, ranks, 0.0), axis=1, keepdims=True)

    def split3(w, first_lane):
        a1 = w.astype(BF16).astype(F32)
        a2 = (w - a1).astype(BF16).astype(F32)
        a3 = (w - a1 - a2).astype(BF16).astype(F32)
        return jnp.where(lane == first_lane, a1, jnp.where(lane == first_lane + 1, a2,
                         jnp.where(lane == first_lane + 2, a3, 0.0)))

    meta = jnp.where(lane == META_EXPERT, i1, jnp.where(lane == META_EXPERT + 1, i2, 0.0))
    meta = meta + jnp.where(lane == META_RANK, lr1, jnp.where(lane == META_RANK + 1, lr2, 0.0))
    meta_ref[...] = meta + split3(w1, META_GATE[0]) + split3(w2, META_GATE[1])
    cnt_ref[...] = jnp.sum(onehot, axis=0, keepdims=True)


def _routing_tables(counts, n_exp_tiles):
    cnt = counts[:, 0, :N_EXPERTS].astype(jnp.int32)
    seg = (cnt + SEG_ALIGN - 1) // SEG_ALIGN * SEG_ALIGN
    loff = jnp.cumsum(seg, axis=1) - seg
    tot = jnp.sum(seg, axis=0)
    gsize = (tot + EXP_TILE - 1) // EXP_TILE * EXP_TILE
    gend = jnp.cumsum(gsize)
    gstart = gend - gsize
    sstart = gstart[None, :] + jnp.cumsum(seg, axis=0) - seg
    n_used = gend[-1] // EXP_TILE
    tile = jnp.arange(n_exp_tiles, dtype=jnp.int32)
    expert = jnp.sum((tile[:, None] * EXP_TILE >= gend[None, :]).astype(jnp.int32), axis=1)
    expert = jnp.where(tile < n_used, expert, expert[n_used - 1])
    flat = lambda a: a.reshape(-1).astype(jnp.int32)
    return dict(seg=flat(seg), loff=flat(loff), sstart=flat(sstart),
                pad_start=flat(jnp.concatenate([gstart + tot, gend[-1:]])),
                pad_len=flat(gsize - tot), expert=flat(expert), n_used=flat(n_used))


def _chunks(n_rows, bits):
    n = n_rows // SEG_ALIGN
    for b in range(bits):
        yield ((n >> b) & 1) == 1, (n & ((1 << b) - 1)) * SEG_ALIGN, SEG_ALIGN << b


def _segment_copies(i, seg_s, loff_s, sstart_s, make):
    for e in range(N_EXPERTS):
        k = i * N_EXPERTS + e
        for cond, off, size in _chunks(seg_s[k], SEG_BITS):
            yield cond, make(pl.multiple_of(loff_s[k] + off, SEG_ALIGN),
                             pl.multiple_of(sstart_s[k] + off, SEG_ALIGN), size)


def _start_copies(plan):
    for cond, copies in plan:
        @pl.when(cond)
        def _():
            for c in copies:
                c.start()


def _wait_copies(plan):
    for cond, copies in plan:
        @pl.when(cond)
        def _():
            for c in copies:
                c.wait()


def _local_slot(expert, rank, loff_s, i):
    off = jnp.zeros_like(rank)
    for e in range(N_EXPERTS):
        off = jnp.where(expert == e, loff_s[i * N_EXPERTS + e].astype(F32), off)
    return (off + rank).astype(jnp.int32)


def _dispatch_kernel(seg_s, loff_s, sstart_s, pstart_s, plen_s, h_ref, meta_ref, xs_hbm, gs_hbm,
                     xloc, gloc, zx, zg, sem):
    i = pl.program_id(0)
    last = pl.num_programs(0) - 1
    tm = h_ref.shape[0]
    buf = i % 2
    meta = meta_ref[...]
    meta_t = meta.T
    slot = lax.broadcasted_iota(jnp.int32, (LOCAL_SLOTS, tm), 0)
    p1, p2 = (slot == _local_slot(meta_t[META_EXPERT + c:META_EXPERT + c + 1, :],
                                  meta_t[META_RANK + c:META_RANK + c + 1, :], loff_s, i) for c in range(2))
    xloc[buf] = jnp.dot(jnp.where(p1 | p2, 1.0, 0.0).astype(BF16), h_ref[...], preferred_element_type=F32)
    lane = lax.broadcasted_iota(jnp.int32, meta.shape, 1)
    gates = [jnp.where((lane >= first) & (lane < first + 3), meta, 0.0).astype(BF16) for first in META_GATE]
    gloc[buf] = (jnp.dot(jnp.where(p1, 1.0, 0.0).astype(BF16), gates[0], preferred_element_type=F32)
                 + jnp.dot(jnp.where(p2, 1.0, 0.0).astype(BF16), gates[1], preferred_element_type=F32))

    def plan(tile, b):
        def make(local, dst, size):
            return [pltpu.make_async_copy(xloc.at[b, pl.ds(local, size)], xs_hbm.at[pl.ds(dst, size)], sem.at[b, 0]),
                    pltpu.make_async_copy(gloc.at[b, pl.ds(local, size)], gs_hbm.at[pl.ds(dst, size)], sem.at[b, 1])]
        return list(_segment_copies(tile, seg_s, loff_s, sstart_s, make))

    _start_copies(plan(i, buf))

    @pl.when(i > 0)
    def _():
        _wait_copies(plan(i - 1, 1 - buf))

    @pl.when(i == last)
    def _():
        _wait_copies(plan(i, buf))
        zx[...] = jnp.zeros_like(zx)
        zg[...] = jnp.zeros_like(zg)
        fill = []
        for e in range(N_EXPERTS):
            for cond, off, size in _chunks(plen_s[e], PAD_BITS):
                dst = pl.multiple_of(pstart_s[e] + off, SEG_ALIGN)
                fill.append((cond, [
                    pltpu.make_async_copy(zx.at[pl.ds(0, size)], xs_hbm.at[pl.ds(dst, size)], sem.at[0, 0]),
                    pltpu.make_async_copy(zg.at[pl.ds(0, size)], gs_hbm.at[pl.ds(dst, size)], sem.at[0, 1])]))
        _start_copies(fill)
        _wait_copies(fill)

        pad_rows = zx.shape[0]
        tail = pstart_s[N_EXPERTS]

        def zero_chunk(c, _):
            dst = pl.multiple_of(tail + c * pad_rows, pad_rows)
            copies = [pltpu.make_async_copy(zx, xs_hbm.at[pl.ds(dst, pad_rows)], sem.at[0, 0]),
                      pltpu.make_async_copy(zg, gs_hbm.at[pl.ds(dst, pad_rows)], sem.at[0, 1])]
            for copy in copies:
                copy.start()
            for copy in copies:
                copy.wait()
            return 0

        lax.fori_loop(0, (xs_hbm.shape[0] - tail) // pad_rows, zero_chunk, 0)


def _dispatch(h, meta, tables, n_slots):
    n = h.shape[0]
    tm = ROW_TILE
    row = lambda width: pl.BlockSpec((tm, width), lambda i, *_: (i, 0))
    pad_rows = SEG_ALIGN << (PAD_BITS - 1)
    return pl.pallas_call(
        _dispatch_kernel,
        grid_spec=pltpu.PrefetchScalarGridSpec(
            num_scalar_prefetch=5, grid=(n // tm,),
            in_specs=[row(D_MODEL), row(LANES)],
            out_specs=[pl.BlockSpec(memory_space=pl.ANY)] * 2,
            scratch_shapes=[pltpu.VMEM((2, LOCAL_SLOTS, D_MODEL), F32), pltpu.VMEM((2, LOCAL_SLOTS, LANES), F32),
                            pltpu.VMEM((pad_rows, D_MODEL), F32), pltpu.VMEM((pad_rows, LANES), F32),
                            pltpu.SemaphoreType.DMA((2, 2))]),
        out_shape=[jax.ShapeDtypeStruct((n_slots, D_MODEL), F32), jax.ShapeDtypeStruct((n_slots, LANES), F32)],
        compiler_params=pltpu.CompilerParams(dimension_semantics=("arbitrary",), vmem_limit_bytes=VMEM_LIMIT),
        name="dispatch",
    )(tables["seg"], tables["loff"], tables["sstart"], tables["pad_start"], tables["pad_len"], h, meta)


def _expert_kernel(expert_s, n_used_s, x_ref, g_ref, wg_ref, wu_ref, wd_ref, y_ref):
    used = pl.program_id(0) < n_used_s[0]

    @pl.when(used)
    def _():
        gate = jnp.sum(g_ref[...], axis=1, keepdims=True)
        y_ref[...] = gate * _swiglu_acc(x_ref[...].astype(BF16), wg_ref, wu_ref, wd_ref)

    @pl.when(jnp.logical_not(used))
    def _():
        y_ref[...] = jnp.zeros_like(y_ref)


def _experts(xs, gs, tables, wg, wu, wd):
    n_slots = xs.shape[0]
    used = lambda i, expert, n_used: (jnp.minimum(i, n_used[0] - 1), 0)
    weight = lambda shape: pl.BlockSpec((None,) + shape, lambda i, expert, n_used: (expert[i], 0, 0))
    return pl.pallas_call(
        _expert_kernel,
        grid_spec=pltpu.PrefetchScalarGridSpec(
            num_scalar_prefetch=2, grid=(n_slots // EXP_TILE,),
            in_specs=[pl.BlockSpec((EXP_TILE, D_MODEL), used), pl.BlockSpec((EXP_TILE, LANES), used),
                      weight((D_MODEL, D_FF)), weight((D_MODEL, D_FF)), weight((D_FF, D_MODEL))],
            out_specs=pl.BlockSpec((EXP_TILE, D_MODEL), lambda i, expert, n_used: (i, 0))),
        out_shape=jax.ShapeDtypeStruct((n_slots, D_MODEL), F32),
        compiler_params=pltpu.CompilerParams(dimension_semantics=("arbitrary",), vmem_limit_bytes=VMEM_LIMIT),
        name="experts",
    )(tables["expert"], tables["n_used"], xs, gs, wg, wu, wd)


def _combine_kernel(seg_s, loff_s, sstart_s, x_ref, meta_ref, y_hbm, o_ref, yloc, sem):
    i = pl.program_id(0)
    tm = x_ref.shape[0]
    buf = i % 2

    def plan(tile, b):
        def make(local, src, size):
            return [pltpu.make_async_copy(y_hbm.at[pl.ds(src, size)], yloc.at[b, pl.ds(local, size)], sem.at[b])]
        return list(_segment_copies(tile, seg_s, loff_s, sstart_s, make))

    @pl.when(i == 0)
    def _():
        yloc[...] = jnp.zeros_like(yloc)
        _start_copies(plan(0, 0))

    @pl.when(i + 1 < pl.num_programs(0))
    def _():
        _start_copies(plan(i + 1, 1 - buf))

    meta = meta_ref[...]
    slot = lax.broadcasted_iota(jnp.int32, (tm, LOCAL_SLOTS), 1)
    p1, p2 = (slot == _local_slot(meta[:, META_EXPERT + c:META_EXPERT + c + 1],
                                  meta[:, META_RANK + c:META_RANK + c + 1], loff_s, i) for c in range(2))
    pick = jnp.where(p1 | p2, 1.0, 0.0).astype(BF16)
    _wait_copies(plan(i, buf))
    hi, lo = _split2(yloc[buf])
    o_ref[...] = (x_ref[...] + jnp.dot(pick, hi, preferred_element_type=F32)
                  + jnp.dot(pick, lo, preferred_element_type=F32))


def _combine(x, meta, ys, tables):
    n = x.shape[0]
    tm = ROW_TILE
    row = lambda width: pl.BlockSpec((tm, width), lambda i, *_: (i, 0))
    return pl.pallas_call(
        _combine_kernel,
        grid_spec=pltpu.PrefetchScalarGridSpec(
            num_scalar_prefetch=3, grid=(n // tm,),
            in_specs=[row(D_MODEL), row(LANES), pl.BlockSpec(memory_space=pl.ANY)],
            out_specs=row(D_MODEL),
            scratch_shapes=[pltpu.VMEM((2, LOCAL_SLOTS, D_MODEL), F32), pltpu.SemaphoreType.DMA((2,))]),
        out_shape=jax.ShapeDtypeStruct((n, D_MODEL), F32),
        compiler_params=pltpu.CompilerParams(dimension_semantics=("arbitrary",), vmem_limit_bytes=VMEM_LIMIT),
        name="combine",
    )(tables["seg"], tables["loff"], tables["sstart"], x, meta, ys)


def _moe(x, h, meta, counts, wg, wu, wd):
    n = x.shape[0]
    n_tiles = n // ROW_TILE
    n_slots = 2 * n + n_tiles * N_EXPERTS * (SEG_ALIGN - 1) + N_EXPERTS * EXP_TILE
    n_slots = -(-n_slots // EXP_TILE) * EXP_TILE
    tables = _routing_tables(counts, n_slots // EXP_TILE)
    xs, gs = _dispatch(h, meta, tables, n_slots)
    ys = _experts(xs, gs, tables, wg, wu, wd)
    return _combine(x, meta, ys, tables)


def _pad_lanes(a, width=LANES):
    return jnp.pad(a, ((0, 0), (0, width - a.shape[1])))


def kernel(x, mem, g_mix, w_in, conv_w, b_f, g_q, g_k, g_conv_out, g_fox_out, w_out, g_xa, g_mem, w_xq, w_xkv,
           g_xq, g_xk, w_xo, g_ffn, w_gate, w_up, w_down, w_router, b_router, we_gate, we_up, we_down):
    batch, seq, _ = x.shape
    n_mem = mem.shape[1]
    xs = x.reshape(batch * seq, D_MODEL)
    mems = mem.reshape(batch * n_mem, D_MODEL)
    n_main = 3 * CONV_CH + 3 * FOX_WIDTH

    group = jnp.arange(MXU_DIM) // HEAD_DIM
    gmat = (group[:, None] == group[None, :]).astype(BF16)
    tri = jnp.tril(jnp.ones((ROW_TILE, ROW_TILE), BF16))
    place = _aug_placement()
    row1 = lambda a: a.reshape(1, -1)

    for l in range(DEPTH):
        w_f = _pad_lanes(jnp.tile(w_in[l][:, n_main:], (1, 3))).astype(BF16)
        b_f3 = _pad_lanes(jnp.tile(row1(b_f[l]), (1, 3)))
        yc, q, k0, k1, v0, v1 = _mix_in(
            xs, row1(g_mix[l]), w_in[l][:, :n_main].astype(BF16), w_f, conv_w[l], b_f3,
            jnp.tile(row1(g_q[l]), (1, FOX_HEADS)), jnp.tile(row1(g_k[l]), (1, FOX_HEADS)),
            row1(g_conv_out[l]), gmat, tri, place, seq)
        yf = _fox_attention(q, k0, k1, v0, v1, row1(g_fox_out[l]), batch, seq)
        kx, vx = _mem_kv(mems, row1(g_mem[l]), w_xkv[l].astype(BF16), row1(g_xk[l]))
        i = l // 2
        router = ()
        if l % 2 == 1:
            wr = _pad_lanes(w_router[i])
            wr_hi = wr.astype(BF16)
            wr_lo = (wr - wr_hi.astype(F32)).astype(BF16)
            router = (row1(g_ffn[l]), wr_hi, wr_lo, _pad_lanes(row1(b_router[i])), tri)
        res = _post(xs, yc, yf, w_out[l].astype(BF16), row1(g_xa[l]), w_xq[l].astype(BF16), row1(g_xq[l]),
                    kx, vx, w_xo[l].astype(BF16), seq, n_mem, router)
        if l % 2 == 0:
            xs = _ffn(res[0], row1(g_ffn[l]), w_gate[i].astype(BF16), w_up[i].astype(BF16), w_down[i].astype(BF16))
        else:
            xs = _moe(*res, we_gate[i].astype(BF16), we_up[i].astype(BF16), we_down[i].astype(BF16))
    return xs.reshape(batch, seq, D_MODEL)
```

```python
import functools

import jax
import jax.numpy as jnp
import numpy as np
from jax import lax
from jax.experimental import pallas as pl
from jax.experimental.pallas import tpu as pltpu

D_MODEL = 1024
DEPTH = 4
HEAD_DIM = 64
CONV_CH = 512
CONV_K = 3
FOX_HEADS = 8
FOX_WIDTH = 512
XA_HEADS = 4
XA_HEAD_DIM = 128
XA_WIDTH = 512
D_FF = 2816
N_EXPERTS = 8
EPS = 1e-6

LANES = 128
SUBLANES = 8
MXU_DIM = 256
VMEM_LIMIT = 56 * 1024 * 1024

ROW_TILE = 512
ATT_TILE = 1024
DIAG_BANDS = 2
FF_CHUNK = 256
EXP_TILE = 512
SEG_ALIGN = SUBLANES
SEG_BITS = (ROW_TILE // SEG_ALIGN).bit_length()
PAD_BITS = (EXP_TILE // SEG_ALIGN - 1).bit_length()
LOCAL_SLOTS = -(-(2 * ROW_TILE + N_EXPERTS * (SEG_ALIGN - 1)) // LANES) * LANES
META_EXPERT, META_RANK, META_GATE = 0, 4, (8, 16)

F32 = jnp.float32
BF16 = jnp.bfloat16


def _const_spec(shape):
    return pl.BlockSpec(shape, lambda *_: (0,) * len(shape), pipeline_mode=pl.Buffered(1))


def _rms(x, g):
    return x * lax.rsqrt(jnp.mean(x * x, axis=-1, keepdims=True) + EPS) * g


def _split2(x):
    hi = x.astype(BF16)
    lo = (x - hi.astype(F32)).astype(BF16)
    return hi, lo


def _group_rms(y, g, gmat):
    sq = (y * y).astype(BF16)
    parts = [jnp.dot(sq[:, c * MXU_DIM:(c + 1) * MXU_DIM], gmat, preferred_element_type=F32)
             for c in range(y.shape[1] // MXU_DIM)]
    ms = jnp.concatenate(parts, axis=1) * (1.0 / HEAD_DIM)
    return y * lax.rsqrt(ms + EPS) * g


def _mix_in_kernel(tiles_per_seq, x_ref, g_ref, w_ref, wf_ref, cw_ref, bf_ref, gq_ref, gk_ref, gc_ref,
                   gmat_ref, tri_ref, place_ref, yc_ref, q_ref, k0_ref, k1_ref, v0_ref, v1_ref, ubuf, ccarry):
    i = pl.program_id(0)
    tm = x_ref.shape[0]
    h = _rms(x_ref[...], g_ref[...]).astype(BF16)

    def proj(j):
        return jnp.dot(h, w_ref[:, j * CONV_CH:(j + 1) * CONV_CH], preferred_element_type=F32)

    @pl.when(i % tiles_per_seq == 0)
    def _():
        ubuf[0:SUBLANES, :] = jnp.zeros((SUBLANES, CONV_CH), F32)
        ccarry[...] = jnp.zeros_like(ccarry)

    ubuf[SUBLANES:SUBLANES + tm, :] = proj(1) * proj(2)
    y = (cw_ref[0:1, :] * ubuf[SUBLANES - 2:SUBLANES - 2 + tm, :]
         + cw_ref[1:2, :] * ubuf[SUBLANES - 1:SUBLANES - 1 + tm, :]
         + cw_ref[2:3, :] * ubuf[SUBLANES:SUBLANES + tm, :])
    ubuf[0:SUBLANES, :] = ubuf[tm:tm + SUBLANES, :]
    gmat = gmat_ref[...]
    yc_ref[...] = _group_rms(proj(0) * y, gc_ref[...], gmat).astype(BF16)

    q_ref[...] = (_group_rms(proj(3), gq_ref[...], gmat) * (HEAD_DIM ** -0.5)).astype(BF16)

    z = jnp.dot(h, wf_ref[...], preferred_element_type=F32) + bf_ref[...]
    logf = jnp.minimum(z, 0.0) - jnp.log1p(jnp.exp(-jnp.abs(z)))
    p1 = logf.astype(BF16)
    r1 = logf - p1.astype(F32)
    p2 = r1.astype(BF16)
    p3 = (r1 - p2.astype(F32)).astype(BF16)
    lane = lax.broadcasted_iota(jnp.int32, logf.shape, 1)
    parts = jnp.where(lane < FOX_HEADS, p1, jnp.where(lane < 2 * FOX_HEADS, p2, p3))
    cs = jnp.dot(tri_ref[...], parts, preferred_element_type=F32)
    cs = cs + pltpu.roll(cs, LANES - FOX_HEADS, axis=1) + pltpu.roll(cs, LANES - 2 * FOX_HEADS, axis=1)
    c = jnp.where(lane < FOX_HEADS, cs + ccarry[...], 0.0)
    ccarry[...] = c[tm - 1:tm, :]

    c3 = -(c + pltpu.roll(c, FOX_HEADS, axis=1) + pltpu.roll(c, 2 * FOX_HEADS, axis=1))
    n1 = c3.astype(BF16)
    r1 = c3 - n1.astype(F32)
    n2 = r1.astype(BF16)
    n3 = (r1 - n2.astype(F32)).astype(BF16)
    nparts = jnp.where(lane < FOX_HEADS, n1, jnp.where(lane < 2 * FOX_HEADS, n2, n3))
    kn = _group_rms(proj(4), gk_ref[...], gmat)
    vv = proj(5)
    lane_w = lax.broadcasted_iota(jnp.int32, (1, FOX_WIDTH), 1)
    for j, (k_ref, v_ref) in enumerate(((k0_ref, v0_ref), (k1_ref, v1_ref))):
        own = (lane_w & HEAD_DIM) == j * HEAD_DIM
        aug = jnp.dot(nparts, place_ref[j], preferred_element_type=F32)
        k_ref[...] = jnp.where(own, kn, aug).astype(BF16)
        ones_lane = (lane_w & (LANES - 1)) == (1 - j) * HEAD_DIM
        v_ref[...] = jnp.where(own, vv, jnp.where(ones_lane, 1.0, 0.0)).astype(BF16)


def _aug_placement():
    place = np.zeros((2, LANES, FOX_WIDTH), np.float32)
    for head in range(FOX_HEADS):
        pair, j = divmod(head, 2)
        for m in range(3):
            place[j, FOX_HEADS * m + head, pair * LANES + (1 - j) * HEAD_DIM + m] = 1.0
    return jnp.asarray(place, BF16)


def _mix_in(x, g, w_main, w_f, conv_w, b_f, gq, gk, gc, gmat, tri, place, seq):
    n = x.shape[0]
    tm = ROW_TILE
    row = lambda width: pl.BlockSpec((tm, width), lambda i: (i, 0))
    return pl.pallas_call(
        functools.partial(_mix_in_kernel, seq // tm),
        grid=(n // tm,),
        in_specs=[row(D_MODEL), _const_spec((1, D_MODEL)), _const_spec(w_main.shape), _const_spec(w_f.shape),
                  _const_spec(conv_w.shape), _const_spec((1, LANES)), _const_spec((1, CONV_CH)),
                  _const_spec((1, CONV_CH)), _const_spec((1, CONV_CH)), _const_spec(gmat.shape),
                  _const_spec(tri.shape), _const_spec(place.shape)],
        out_specs=[row(CONV_CH)] * 6,
        out_shape=[jax.ShapeDtypeStruct((n, CONV_CH), BF16)] * 6,
        scratch_shapes=[pltpu.VMEM((tm + SUBLANES, CONV_CH), F32), pltpu.VMEM((1, LANES), F32)],
        compiler_params=pltpu.CompilerParams(dimension_semantics=("arbitrary",), vmem_limit_bytes=VMEM_LIMIT),
        name="mix_in",
    )(x, g, w_main, w_f, conv_w, b_f, gq, gk, gc, gmat, tri, place)


def _fox_kernel(q_ref, k0_ref, k1_ref, v0_ref, v1_ref, g_ref, o_ref, sa_scr, sb_scr, m_scr, acc_scr):
    tq = tk = ATT_TILE
    nq = q_ref.shape[0] // tq
    k_refs = (k0_ref, k1_ref)
    v_refs = (v0_ref, v1_ref)
    bufs = (sa_scr, sb_scr)
    lane = lax.broadcasted_iota(jnp.int32, (1, LANES), 1)
    own = [(lane & HEAD_DIM) == j * HEAD_DIM for j in range(2)]
    spare = [(1 - j) * HEAD_DIM for j in range(2)]
    ones = [jnp.where((lane >= spare[j]) & (lane < spare[j] + 3), 1.0, 0.0).astype(BF16) for j in range(2)]
    causal = lax.broadcasted_iota(jnp.int32, (tq, tk), 0) >= lax.broadcasted_iota(jnp.int32, (tq, tk), 1)
    gain = g_ref[...]

    band = tq // DIAG_BANDS
    bands = [(slice(r * band, (r + 1) * band), (r + 1) * band) for r in range(DIAG_BANDS)]

    def scores(qi, kv, s_scr):
        q = q_ref[qi * tq:(qi + 1) * tq, :]
        start = kv * tk if isinstance(kv, int) else pl.multiple_of(kv * tk, tk)
        parts = bands if isinstance(kv, int) and kv == qi else [(slice(0, tq), tk)]
        for j in range(2):
            qa = jnp.where(own[j], q, ones[j])
            for rows, keys in parts:
                s_scr[j, rows, 0:keys] = lax.dot_general(qa[rows], k_refs[j][pl.ds(start, keys), :],
                                                         (((1,), (1,)), ((), ())), preferred_element_type=F32)

    def absorb(kv, s_scr, carry, on_diagonal=False):
        start = kv * tk if isinstance(kv, int) else pl.multiple_of(kv * tk, tk)
        parts = bands if on_diagonal else [(slice(0, tq), tk)]
        new = []
        for j in range(2):
            m, acc = carry[j]
            m_parts, acc_parts = [], []
            for rows, keys in parts:
                s = s_scr[j, rows, 0:keys]
                if on_diagonal:
                    s = jnp.where(causal[rows, 0:keys], s, -jnp.inf)
                m_new = jnp.maximum(m[rows], jnp.max(s, axis=1, keepdims=True))
                p = jnp.exp(s - m_new).astype(BF16)
                acc_parts.append(jnp.exp(m[rows] - m_new) * acc[rows]
                                 + jnp.dot(p, v_refs[j][pl.ds(start, keys), :], preferred_element_type=F32))
                m_parts.append(m_new)
            new.append((jnp.concatenate(m_parts, axis=0), jnp.concatenate(acc_parts, axis=0)))
        return tuple(new)

    def load_state():
        return tuple((m_scr[j], acc_scr[j]) for j in range(2))

    def store_state(carry):
        for j in range(2):
            m_scr[j], acc_scr[j] = carry[j]

    def finish(qi, carry):
        out = None
        for j in range(2):
            acc = carry[j][1]
            l = jnp.sum(jnp.where(lane == spare[j], acc, 0.0), axis=1, keepdims=True)
            o = jnp.where(own[j], acc, 0.0)
            o = o * lax.rsqrt(jnp.sum(o * o, axis=1, keepdims=True) * (1.0 / HEAD_DIM) + EPS * (l * l))
            out = o if out is None else out + o
        o_ref[qi * tq:(qi + 1) * tq, :] = (out * gain).astype(BF16)

    init = tuple((jnp.full((tq, 1), -jnp.inf, F32), jnp.zeros((tq, LANES), F32)) for _ in range(2))
    t0 = 0
    scores(0, 0, bufs[0])
    for qi in range(nq):
        first, second = bufs[t0 % 2], bufs[(t0 + 1) % 2]

        def two_blocks(kv, carry, qi=qi, first=first, second=second):
            scores(qi, kv + 1, second)
            carry = absorb(kv, first, carry)
            scores(qi, kv + 2, first)
            return absorb(kv + 1, second, carry)

        def two_blocks_in_place(u, _, two_blocks=two_blocks):
            store_state(two_blocks(2 * u, load_state()))
            return 0

        carry = init
        if qi // 2 == 1:
            carry = two_blocks(0, carry)
        elif qi // 2 > 1:
            store_state(carry)
            lax.fori_loop(0, qi // 2, two_blocks_in_place, 0)
            carry = load_state()
        if qi % 2 == 1:
            scores(qi, qi, second)
            carry = absorb(qi - 1, first, carry)
            diagonal, free = second, first
        else:
            diagonal, free = first, second
        if qi + 1 < nq:
            scores(qi + 1, 0, free)
        finish(qi, absorb(qi, diagonal, carry, on_diagonal=True))
        t0 += qi + 1


def _fox_attention(q, k0, k1, v0, v1, g_fox, batch, seq):
    n = q.shape[0]
    spec = pl.BlockSpec((seq, LANES), lambda b, p: (b, p))
    return pl.pallas_call(
        _fox_kernel,
        grid=(batch, FOX_HEADS // 2),
        in_specs=[spec] * 5 + [pl.BlockSpec((1, LANES), lambda b, p: (0, p))],
        out_specs=spec,
        out_shape=jax.ShapeDtypeStruct((n, FOX_WIDTH), BF16),
        scratch_shapes=[pltpu.VMEM((2, ATT_TILE, ATT_TILE), F32)] * 2
                       + [pltpu.VMEM((2, ATT_TILE, 1), F32), pltpu.VMEM((2, ATT_TILE, LANES), F32)],
        compiler_params=pltpu.CompilerParams(dimension_semantics=("arbitrary",) * 2, vmem_limit_bytes=VMEM_LIMIT),
        name="fox_attn",
    )(q, k0, k1, v0, v1, g_fox)


def _mem_kv_kernel(m_ref, g_ref, w_ref, gk_ref, k_ref, v_ref):
    h = _rms(m_ref[...], g_ref[...]).astype(BF16)
    kv = jnp.dot(h, w_ref[...], preferred_element_type=F32)
    gk = gk_ref[...]
    ks = [_rms(kv[:, a * XA_HEAD_DIM:(a + 1) * XA_HEAD_DIM], gk) for a in range(XA_HEADS)]
    k_ref[...] = jnp.concatenate(ks, axis=1).astype(BF16)
    v_ref[...] = kv[:, XA_WIDTH:].astype(BF16)


def _mem_kv(mem, g, w_xkv, g_xk):
    n = mem.shape[0]
    tm = ROW_TILE
    row = lambda width: pl.BlockSpec((tm, width), lambda i: (i, 0))
    return pl.pallas_call(
        _mem_kv_kernel,
        grid=(n // tm,),
        in_specs=[row(D_MODEL), _const_spec((1, D_MODEL)), _const_spec(w_xkv.shape), _const_spec((1, XA_HEAD_DIM))],
        out_specs=[row(XA_WIDTH)] * 2,
        out_shape=[jax.ShapeDtypeStruct((n, XA_WIDTH), BF16)] * 2,
        compiler_params=pltpu.CompilerParams(dimension_semantics=("arbitrary",), vmem_limit_bytes=VMEM_LIMIT),
        name="mem_kv",
    )(mem, g, w_xkv, g_xk)


def _post_kernel(x_ref, yc_ref, yf_ref, wo_ref, g_ref, wq_ref, gq_ref, k_ref, v_ref, wxo_ref, *rest):
    o_ref = rest[-4] if len(rest) > 1 else rest[0]
    x1 = (x_ref[...]
          + jnp.dot(yc_ref[...], wo_ref[0:CONV_CH, :], preferred_element_type=F32)
          + jnp.dot(yf_ref[...], wo_ref[CONV_CH:, :], preferred_element_type=F32))
    hx = _rms(x1, g_ref[...]).astype(BF16)
    qx = jnp.dot(hx, wq_ref[...], preferred_element_type=F32)
    gq = gq_ref[...]
    heads = []
    for a in range(XA_HEADS):
        sl = slice(a * XA_HEAD_DIM, (a + 1) * XA_HEAD_DIM)
        qa = _rms(qx[:, sl], gq).astype(BF16)
        s = lax.dot_general(qa, k_ref[:, sl], (((1,), (1,)), ((), ())),
                            preferred_element_type=F32) * (XA_HEAD_DIM ** -0.5)
        e = jnp.exp(s - jnp.max(s, axis=1, keepdims=True))
        p = (e / jnp.sum(e, axis=1, keepdims=True)).astype(BF16)
        heads.append(jnp.dot(p, v_ref[:, sl], preferred_element_type=F32).astype(BF16))
    att = jnp.concatenate(heads, axis=1)
    x2 = x1 + jnp.dot(att, wxo_ref[...], preferred_element_type=F32)
    o_ref[...] = x2
    if len(rest) > 1:
        _route_tile(x2, *rest[:-4], *rest[-3:])


def _post(x, yc, yf, w_out, g_xa, w_xq, g_xq, kx, vx, w_xo, seq, n_mem, router=()):
    n = x.shape[0]
    tm = ROW_TILE
    tiles_per_seq = seq // tm
    row = lambda width: pl.BlockSpec((tm, width), lambda i: (i, 0))
    memspec = pl.BlockSpec((n_mem, XA_WIDTH), lambda i: (i // tiles_per_seq, 0))
    out_specs, out_shape = [row(D_MODEL)], [jax.ShapeDtypeStruct((n, D_MODEL), F32)]
    if router:
        out_specs += [row(D_MODEL), row(LANES), pl.BlockSpec((None, 1, LANES), lambda i: (i, 0, 0))]
        out_shape += [jax.ShapeDtypeStruct((n, D_MODEL), BF16), jax.ShapeDtypeStruct((n, LANES), F32),
                      jax.ShapeDtypeStruct((n // tm, 1, LANES), F32)]
    return pl.pallas_call(
        _post_kernel,
        grid=(n // tm,),
        in_specs=[row(D_MODEL), row(CONV_CH), row(FOX_WIDTH), _const_spec(w_out.shape), _const_spec((1, D_MODEL)),
                  _const_spec(w_xq.shape), _const_spec((1, XA_HEAD_DIM)), memspec, memspec, _const_spec(w_xo.shape)]
                 + [_const_spec(a.shape) for a in router],
        out_specs=out_specs,
        out_shape=out_shape,
        compiler_params=pltpu.CompilerParams(dimension_semantics=("arbitrary",), vmem_limit_bytes=VMEM_LIMIT),
        name="post",
    )(x, yc, yf, w_out, g_xa, w_xq, g_xq, kx, vx, w_xo, *router)


def _swiglu_acc(h, wg_ref, wu_ref, wd_ref):
    acc = None
    for c in range(D_FF // FF_CHUNK):
        sl = slice(c * FF_CHUNK, (c + 1) * FF_CHUNK)
        gate = jnp.dot(h, wg_ref[:, sl], preferred_element_type=F32)
        up = jnp.dot(h, wu_ref[:, sl], preferred_element_type=F32)
        a = (gate * (1.0 / (1.0 + jnp.exp(-gate))) * up).astype(BF16)
        part = jnp.dot(a, wd_ref[sl, :], preferred_element_type=F32)
        acc = part if acc is None else acc + part
    return acc


def _ffn_kernel(x_ref, g_ref, wg_ref, wu_ref, wd_ref, o_ref):
    x = x_ref[...]
    h = _rms(x, g_ref[...]).astype(BF16)
    o_ref[...] = x + _swiglu_acc(h, wg_ref, wu_ref, wd_ref)


def _ffn(x, g, wg, wu, wd):
    n = x.shape[0]
    tm = ROW_TILE
    row = pl.BlockSpec((tm, D_MODEL), lambda i: (i, 0))
    return pl.pallas_call(
        _ffn_kernel,
        grid=(n // tm,),
        in_specs=[row, _const_spec((1, D_MODEL)), _const_spec(wg.shape), _const_spec(wu.shape),
                  _const_spec(wd.shape)],
        out_specs=row,
        out_shape=jax.ShapeDtypeStruct((n, D_MODEL), F32),
        compiler_params=pltpu.CompilerParams(dimension_semantics=("arbitrary",), vmem_limit_bytes=VMEM_LIMIT),
        name="ffn",
    )(x, g, wg, wu, wd)


def _route_tile(x, g_ref, wr_hi_ref, wr_lo_ref, br_ref, tri_ref, h_ref, meta_ref, cnt_ref):
    lane = lax.broadcasted_iota(jnp.int32, meta_ref.shape, 1).astype(F32)
    hf = _rms(x, g_ref[...])
    h_hi, h_lo = _split2(hf)
    h_ref[...] = h_hi
    logits = (jnp.dot(h_hi, wr_hi_ref[...], preferred_element_type=F32)
              + jnp.dot(h_hi, wr_lo_ref[...], preferred_element_type=F32)
              + jnp.dot(h_lo, wr_hi_ref[...], preferred_element_type=F32)) + br_ref[...]
    logits = jnp.where(lane < N_EXPERTS, logits, -jnp.inf)
    m1 = jnp.max(logits, axis=1, keepdims=True)
    i1 = jnp.min(jnp.where(logits == m1, lane, LANES), axis=1, keepdims=True)
    rest = jnp.where(lane == i1, -jnp.inf, logits)
    m2 = jnp.max(rest, axis=1, keepdims=True)
    i2 = jnp.min(jnp.where(rest == m2, lane, LANES), axis=1, keepdims=True)
    e2 = jnp.exp(m2 - m1)
    w1 = 1.0 / (1.0 + e2)
    w2 = e2 * w1
    onehot = jnp.where((lane == i1) | (lane == i2), 1.0, 0.0)
    ranks = jnp.dot(tri_ref[...], onehot.astype(BF16), preferred_element_type=F32) - onehot
    lr1 = jnp.sum(jnp.where(lane == i1, ranks, 0.0), axis=1, keepdims=True)
    lr2 = jnp.sum(jnp.where(lane == i2, ranks, 0.0), axis=1, keepdims=True)

    def split3(w, first_lane):
        a1 = w.astype(BF16).astype(F32)
        a2 = (w - a1).astype(BF16).astype(F32)
        a3 = (w - a1 - a2).astype(BF16).astype(F32)
        return jnp.where(lane == first_lane, a1, jnp.where(lane == first_lane + 1, a2,
                         jnp.where(lane == first_lane + 2, a3, 0.0)))

    meta = jnp.where(lane == META_EXPERT, i1, jnp.where(lane == META_EXPERT + 1, i2, 0.0))
    meta = meta + jnp.where(lane == META_RANK, lr1, jnp.where(lane == META_RANK + 1, lr2, 0.0))
    meta_ref[...] = meta + split3(w1, META_GATE[0]) + split3(w2, META_GATE[1])
    cnt_ref[...] = jnp.sum(onehot, axis=0, keepdims=True)


def _routing_tables(counts, n_exp_tiles):
    cnt = counts[:, 0, :N_EXPERTS].astype(jnp.int32)
    seg = (cnt + SEG_ALIGN - 1) // SEG_ALIGN * SEG_ALIGN
    loff = jnp.cumsum(seg, axis=1) - seg
    tot = jnp.sum(seg, axis=0)
    gsize = (tot + EXP_TILE - 1) // EXP_TILE * EXP_TILE
    gend = jnp.cumsum(gsize)
    gstart = gend - gsize
    sstart = gstart[None, :] + jnp.cumsum(seg, axis=0) - seg
    n_used = gend[-1] // EXP_TILE
    tile = jnp.arange(n_exp_tiles, dtype=jnp.int32)
    expert = jnp.sum((tile[:, None] * EXP_TILE >= gend[None, :]).astype(jnp.int32), axis=1)
    expert = jnp.where(tile < n_used, expert, expert[n_used - 1])
    flat = lambda a: a.reshape(-1).astype(jnp.int32)
    return dict(seg=flat(seg), loff=flat(loff), sstart=flat(sstart),
                pad_start=flat(jnp.concatenate([gstart + tot, gend[-1:]])),
                pad_len=flat(gsize - tot), expert=flat(expert), n_used=flat(n_used))


def _chunks(n_rows, bits):
    n = n_rows // SEG_ALIGN
    for b in range(bits):
        yield ((n >> b) & 1) == 1, (n & ((1 << b) - 1)) * SEG_ALIGN, SEG_ALIGN << b


def _segment_copies(i, seg_s, loff_s, sstart_s, make):
    for e in range(N_EXPERTS):
        k = i * N_EXPERTS + e
        for cond, off, size in _chunks(seg_s[k], SEG_BITS):
            yield cond, make(pl.multiple_of(loff_s[k] + off, SEG_ALIGN),
                             pl.multiple_of(sstart_s[k] + off, SEG_ALIGN), size)


def _start_copies(plan):
    for cond, copies in plan:
        @pl.when(cond)
        def _():
            for c in copies:
                c.start()


def _wait_copies(plan):
    for cond, copies in plan:
        @pl.when(cond)
        def _():
            for c in copies:
                c.wait()


def _local_slot(expert, rank, loff_s, i):
    off = jnp.zeros_like(rank)
    for e in range(N_EXPERTS):
        off = jnp.where(expert == e, loff_s[i * N_EXPERTS + e].astype(F32), off)
    return (off + rank).astype(jnp.int32)


def _dispatch_kernel(seg_s, loff_s, sstart_s, pstart_s, plen_s, h_ref, meta_ref, xs_hbm, gs_hbm,
                     xloc, gloc, zx, zg, sem):
    i = pl.program_id(0)
    last = pl.num_programs(0) - 1
    tm = h_ref.shape[0]
    buf = i % 2
    meta = meta_ref[...]
    meta_t = meta.T
    slot = lax.broadcasted_iota(jnp.int32, (LOCAL_SLOTS, tm), 0)
    p1, p2 = (slot == _local_slot(meta_t[META_EXPERT + c:META_EXPERT + c + 1, :],
                                  meta_t[META_RANK + c:META_RANK + c + 1, :], loff_s, i) for c in range(2))
    xloc[buf] = jnp.dot(jnp.where(p1 | p2, 1.0, 0.0).astype(BF16), h_ref[...], preferred_element_type=F32)
    lane = lax.broadcasted_iota(jnp.int32, meta.shape, 1)
    gates = [jnp.where((lane >= first) & (lane < first + 3), meta, 0.0).astype(BF16) for first in META_GATE]
    gloc[buf] = (jnp.dot(jnp.where(p1, 1.0, 0.0).astype(BF16), gates[0], preferred_element_type=F32)
                 + jnp.dot(jnp.where(p2, 1.0, 0.0).astype(BF16), gates[1], preferred_element_type=F32))

    def plan(tile, b):
        def make(local, dst, size):
            return [pltpu.make_async_copy(xloc.at[b, pl.ds(local, size)], xs_hbm.at[pl.ds(dst, size)], sem.at[b, 0]),
                    pltpu.make_async_copy(gloc.at[b, pl.ds(local, size)], gs_hbm.at[pl.ds(dst, size)], sem.at[b, 1])]
        return list(_segment_copies(tile, seg_s, loff_s, sstart_s, make))

    _start_copies(plan(i, buf))

    @pl.when(i > 0)
    def _():
        _wait_copies(plan(i - 1, 1 - buf))

    @pl.when(i == last)
    def _():
        _wait_copies(plan(i, buf))
        zx[...] = jnp.zeros_like(zx)
        zg[...] = jnp.zeros_like(zg)
        fill = []
        for e in range(N_EXPERTS):
            for cond, off, size in _chunks(plen_s[e], PAD_BITS):
                dst = pl.multiple_of(pstart_s[e] + off, SEG_ALIGN)
                fill.append((cond, [
                    pltpu.make_async_copy(zx.at[pl.ds(0, size)], xs_hbm.at[pl.ds(dst, size)], sem.at[0, 0]),
                    pltpu.make_async_copy(zg.at[pl.ds(0, size)], gs_hbm.at[pl.ds(dst, size)], sem.at[0, 1])]))
        _start_copies(fill)
        _wait_copies(fill)

        pad_rows = zx.shape[0]
        tail = pstart_s[N_EXPERTS]

        def zero_chunk(c, _):
            dst = pl.multiple_of(tail + c * pad_rows, pad_rows)
            copies = [pltpu.make_async_copy(zx, xs_hbm.at[pl.ds(dst, pad_rows)], sem.at[0, 0]),
                      pltpu.make_async_copy(zg, gs_hbm.at[pl.ds(dst, pad_rows)], sem.at[0, 1])]
            for copy in copies:
                copy.start()
            for copy in copies:
                copy.wait()
            return 0

        lax.fori_loop(0, (xs_hbm.shape[0] - tail) // pad_rows, zero_chunk, 0)


def _dispatch(h, meta, tables, n_slots):
    n = h.shape[0]
    tm = ROW_TILE
    row = lambda width: pl.BlockSpec((tm, width), lambda i, *_: (i, 0))
    pad_rows = SEG_ALIGN << (PAD_BITS - 1)
    return pl.pallas_call(
        _dispatch_kernel,
        grid_spec=pltpu.PrefetchScalarGridSpec(
            num_scalar_prefetch=5, grid=(n // tm,),
            in_specs=[row(D_MODEL), row(LANES)],
            out_specs=[pl.BlockSpec(memory_space=pl.ANY)] * 2,
            scratch_shapes=[pltpu.VMEM((2, LOCAL_SLOTS, D_MODEL), F32), pltpu.VMEM((2, LOCAL_SLOTS, LANES), F32),
                            pltpu.VMEM((pad_rows, D_MODEL), F32), pltpu.VMEM((pad_rows, LANES), F32),
                            pltpu.SemaphoreType.DMA((2, 2))]),
        out_shape=[jax.ShapeDtypeStruct((n_slots, D_MODEL), F32), jax.ShapeDtypeStruct((n_slots, LANES), F32)],
        compiler_params=pltpu.CompilerParams(dimension_semantics=("arbitrary",), vmem_limit_bytes=VMEM_LIMIT),
        name="dispatch",
    )(tables["seg"], tables["loff"], tables["sstart"], tables["pad_start"], tables["pad_len"], h, meta)


def _expert_kernel(expert_s, n_used_s, x_ref, g_ref, wg_ref, wu_ref, wd_ref, y_ref):
    used = pl.program_id(0) < n_used_s[0]

    @pl.when(used)
    def _():
        gate = jnp.sum(g_ref[...], axis=1, keepdims=True)
        y_ref[...] = gate * _swiglu_acc(x_ref[...].astype(BF16), wg_ref, wu_ref, wd_ref)

    @pl.when(jnp.logical_not(used))
    def _():
        y_ref[...] = jnp.zeros_like(y_ref)


def _experts(xs, gs, tables, wg, wu, wd):
    n_slots = xs.shape[0]
    used = lambda i, expert, n_used: (jnp.minimum(i, n_used[0] - 1), 0)
    weight = lambda shape: pl.BlockSpec((None,) + shape, lambda i, expert, n_used: (expert[i], 0, 0))
    return pl.pallas_call(
        _expert_kernel,
        grid_spec=pltpu.PrefetchScalarGridSpec(
            num_scalar_prefetch=2, grid=(n_slots // EXP_TILE,),
            in_specs=[pl.BlockSpec((EXP_TILE, D_MODEL), used), pl.BlockSpec((EXP_TILE, LANES), used),
                      weight((D_MODEL, D_FF)), weight((D_MODEL, D_FF)), weight((D_FF, D_MODEL))],
            out_specs=pl.BlockSpec((EXP_TILE, D_MODEL), lambda i, expert, n_used: (i, 0))),
        out_shape=jax.ShapeDtypeStruct((n_slots, D_MODEL), F32),
        compiler_params=pltpu.CompilerParams(dimension_semantics=("arbitrary",), vmem_limit_bytes=VMEM_LIMIT),
        name="experts",
    )(tables["expert"], tables["n_used"], xs, gs, wg, wu, wd)


def _combine_kernel(seg_s, loff_s, sstart_s, x_ref, meta_ref, y_hbm, o_ref, yloc, sem):
    i = pl.program_id(0)
    tm = x_ref.shape[0]
    buf = i % 2

    def plan(tile, b):
        def make(local, src, size):
            return [pltpu.make_async_copy(y_hbm.at[pl.ds(src, size)], yloc.at[b, pl.ds(local, size)], sem.at[b])]
        return list(_segment_copies(tile, seg_s, loff_s, sstart_s, make))

    @pl.when(i == 0)
    def _():
        yloc[...] = jnp.zeros_like(yloc)
        _start_copies(plan(0, 0))

    @pl.when(i + 1 < pl.num_programs(0))
    def _():
        _start_copies(plan(i + 1, 1 - buf))

    meta = meta_ref[...]
    slot = lax.broadcasted_iota(jnp.int32, (tm, LOCAL_SLOTS), 1)
    p1, p2 = (slot == _local_slot(meta[:, META_EXPERT + c:META_EXPERT + c + 1],
                                  meta[:, META_RANK + c:META_RANK + c + 1], loff_s, i) for c in range(2))
    pick = jnp.where(p1 | p2, 1.0, 0.0).astype(BF16)
    _wait_copies(plan(i, buf))
    hi, lo = _split2(yloc[buf])
    o_ref[...] = (x_ref[...] + jnp.dot(pick, hi, preferred_element_type=F32)
                  + jnp.dot(pick, lo, preferred_element_type=F32))


def _combine(x, meta, ys, tables):
    n = x.shape[0]
    tm = ROW_TILE
    row = lambda width: pl.BlockSpec((tm, width), lambda i, *_: (i, 0))
    return pl.pallas_call(
        _combine_kernel,
        grid_spec=pltpu.PrefetchScalarGridSpec(
            num_scalar_prefetch=3, grid=(n // tm,),
            in_specs=[row(D_MODEL), row(LANES), pl.BlockSpec(memory_space=pl.ANY)],
            out_specs=row(D_MODEL),
            scratch_shapes=[pltpu.VMEM((2, LOCAL_SLOTS, D_MODEL), F32), pltpu.SemaphoreType.DMA((2,))]),
        out_shape=jax.ShapeDtypeStruct((n, D_MODEL), F32),
        compiler_params=pltpu.CompilerParams(dimension_semantics=("arbitrary",), vmem_limit_bytes=VMEM_LIMIT),
        name="combine",
    )(tables["seg"], tables["loff"], tables["sstart"], x, meta, ys)


def _moe(x, h, meta, counts, wg, wu, wd):
    n = x.shape[0]
    n_tiles = n // ROW_TILE
    n_slots = 2 * n + n_tiles * N_EXPERTS * (SEG_ALIGN - 1) + N_EXPERTS * EXP_TILE
    n_slots = -(-n_slots // EXP_TILE) * EXP_TILE
    tables = _routing_tables(counts, n_slots // EXP_TILE)
    xs, gs = _dispatch(h, meta, tables, n_slots)
    ys = _experts(xs, gs, tables, wg, wu, wd)
    return _combine(x, meta, ys, tables)


def _pad_lanes(a, width=LANES):
    return jnp.pad(a, ((0, 0), (0, width - a.shape[1])))


def kernel(x, mem, g_mix, w_in, conv_w, b_f, g_q, g_k, g_conv_out, g_fox_out, w_out, g_xa, g_mem, w_xq, w_xkv,
           g_xq, g_xk, w_xo, g_ffn, w_gate, w_up, w_down, w_router, b_router, we_gate, we_up, we_down):
    batch, seq, _ = x.shape
    n_mem = mem.shape[1]
    xs = x.reshape(batch * seq, D_MODEL)
    mems = mem.reshape(batch * n_mem, D_MODEL)
    n_main = 3 * CONV_CH + 3 * FOX_WIDTH

    group = jnp.arange(MXU_DIM) // HEAD_DIM
    gmat = (group[:, None] == group[None, :]).astype(BF16)
    tri = jnp.tril(jnp.ones((ROW_TILE, ROW_TILE), BF16))
    place = _aug_placement()
    row1 = lambda a: a.reshape(1, -1)

    for l in range(DEPTH):
        w_f = _pad_lanes(jnp.tile(w_in[l][:, n_main:], (1, 3))).astype(BF16)
        b_f3 = _pad_lanes(jnp.tile(row1(b_f[l]), (1, 3)))
        yc, q, k0, k1, v0, v1 = _mix_in(
            xs, row1(g_mix[l]), w_in[l][:, :n_main].astype(BF16), w_f, conv_w[l], b_f3,
            jnp.tile(row1(g_q[l]), (1, FOX_HEADS)), jnp.tile(row1(g_k[l]), (1, FOX_HEADS)),
            row1(g_conv_out[l]), gmat, tri, place, seq)
        yf = _fox_attention(q, k0, k1, v0, v1, row1(g_fox_out[l]), batch, seq)
        kx, vx = _mem_kv(mems, row1(g_mem[l]), w_xkv[l].astype(BF16), row1(g_xk[l]))
        i = l // 2
        router = ()
        if l % 2 == 1:
            wr = _pad_lanes(w_router[i])
            wr_hi = wr.astype(BF16)
            wr_lo = (wr - wr_hi.astype(F32)).astype(BF16)
            router = (row1(g_ffn[l]), wr_hi, wr_lo, _pad_lanes(row1(b_router[i])), tri)
        res = _post(xs, yc, yf, w_out[l].astype(BF16), row1(g_xa[l]), w_xq[l].astype(BF16), row1(g_xq[l]),
                    kx, vx, w_xo[l].astype(BF16), seq, n_mem, router)
        if l % 2 == 0:
            xs = _ffn(res[0], row1(g_ffn[l]), w_gate[i].astype(BF16), w_up[i].astype(BF16), w_down[i].astype(BF16))
        else:
            xs = _moe(*res, we_gate[i].astype(BF16), we_up[i].astype(BF16), we_down[i].astype(BF16))
    return xs.reshape(batch, seq, D_MODEL)
```
